```python
import math
import jax, jax.numpy as jnp
from jax import lax
import numpy as np

D_MODEL = 2048
BATCH = 4
SEQ = 2048
DEPTH = 2
DEC_BATCH = 8
DEC_SEQ = 64
PAST_LEN = 1024

CHUNK = 64
Q_BLOCK = 128
SB_HEADS = 8
SB_HEAD_DIM = D_MODEL // 16
DIFF_HEADS = 8
DIFF_QK_DIM = D_MODEL // 32
DIFF_V_DIM = 2 * DIFF_QK_DIM
D_FF = 4 * D_MODEL
EPS = 1e-6
LAMBDA_STD = 0.1

SB_WIDTH = SB_HEADS * SB_HEAD_DIM
DIFF_QK_WIDTH = DIFF_HEADS * 2 * DIFF_QK_DIM
DIFF_V_WIDTH = DIFF_HEADS * DIFF_V_DIM
IN_COLS = 3 * SB_WIDTH + 2 * DIFF_QK_WIDTH + DIFF_V_WIDTH + 2 * D_MODEL
SPLITS = (SB_WIDTH, 2 * SB_WIDTH, 3 * SB_WIDTH,
          3 * SB_WIDTH + DIFF_QK_WIDTH,
          3 * SB_WIDTH + 2 * DIFF_QK_WIDTH,
          3 * SB_WIDTH + 2 * DIFF_QK_WIDTH + DIFF_V_WIDTH)

kernel_name = "stickbreak_diffattn_gated_hybrid_stream_step"


def rmsnorm(x, g):
    xf = x.astype(jnp.float32)
    y = xf * lax.rsqrt(jnp.mean(xf * xf, axis=-1, keepdims=True) + EPS)
    return (y * g.astype(jnp.float32)).astype(x.dtype)


def alibi_slopes():
    h = jnp.arange(1, DIFF_HEADS + 1, dtype=jnp.float32)
    return jnp.exp2(-8.0 * h / DIFF_HEADS)


def stick_breaking_block(q, q_pos, k, v, k_pos):
    z = jnp.einsum("bqhd,bkhd->bhqk", q, k).astype(jnp.float32) * (SB_HEAD_DIM ** -0.5)
    visible = k_pos[None, :] < q_pos[:, None]
    log_1m = jnp.where(visible, jax.nn.log_sigmoid(-z), 0.0)
    between = lax.cumsum(log_1m, axis=3, reverse=True) - log_1m
    w = jnp.where(visible, jnp.exp(jax.nn.log_sigmoid(z) + between), 0.0)
    return jnp.einsum("bhqk,bkhd->bqhd", w.astype(v.dtype), v)


def diff_attention_block(q, q_pos, k, v, k_pos, lam, slopes):
    s = jnp.einsum("bqhcd,bkhcd->bchqk", q, k).astype(jnp.float32) * (DIFF_QK_DIM ** -0.5)
    dist = jnp.abs(q_pos[:, None] - k_pos[None, :]).astype(jnp.float32)
    visible = (k_pos[None, :] // CHUNK) <= (q_pos[:, None] // CHUNK)
    bias = jnp.where(visible[None], -slopes[:, None, None] * dist[None], -jnp.inf)
    p = jax.nn.softmax(s + bias, axis=-1)
    w = p[:, 0] - lam * p[:, 1]
    return jnp.einsum("bhqk,bkhd->bqhd", w.astype(v.dtype), v)


def sweep_query_blocks(fn, q, q_pos):
    b, t = q.shape[0], q.shape[1]
    nb = t // Q_BLOCK
    qb = jnp.moveaxis(q.reshape((b, nb, Q_BLOCK) + q.shape[2:]), 1, 0)
    pb = q_pos.reshape(nb, Q_BLOCK)
    out = lax.map(lambda a: fn(a[0], a[1]), (qb, pb))
    out = jnp.moveaxis(out, 0, 1)
    return out.reshape((b, t) + out.shape[3:])


def hybrid_layer(x, q_pos, past, layer, norm1_g, w_in, lambda_q1, lambda_k1, lambda_q2, lambda_k2,
                 diff_norm_g, w_branch_sb, w_branch_diff, w_out, norm2_g, w_up, w_down):
    b, t, _ = x.shape
    h = rmsnorm(x, norm1_g)
    proj = h @ w_in
    q_sb, k_sb, v_sb, q_df, k_df, v_df, gates = jnp.split(proj, SPLITS, axis=-1)
    q_sb = q_sb.reshape(b, t, SB_HEADS, SB_HEAD_DIM)
    k_sb = k_sb.reshape(b, t, SB_HEADS, SB_HEAD_DIM)
    v_sb = v_sb.reshape(b, t, SB_HEADS, SB_HEAD_DIM)
    q_df = q_df.reshape(b, t, DIFF_HEADS, 2, DIFF_QK_DIM)
    k_df = k_df.reshape(b, t, DIFF_HEADS, 2 * DIFF_QK_DIM)
    v_df = v_df.reshape(b, t, DIFF_HEADS, DIFF_V_DIM)
    new_rows = (k_sb, v_sb, k_df, v_df)

    if past is None:
        ks, vs, kd, vd = new_rows
        k_pos = q_pos
    else:
        ks, vs, kd, vd = (jnp.concatenate([p, n], axis=1) for p, n in zip(past, new_rows))
        k_pos = jnp.arange(ks.shape[1], dtype=jnp.int32)
    kd = kd.reshape(kd.shape[:3] + (2, DIFF_QK_DIM))

    lam_init = 0.8 - 0.6 * math.exp(-0.3 * layer)
    lam = (jnp.exp(jnp.sum(lambda_q1.astype(jnp.float32) * lambda_k1.astype(jnp.float32)))
           - jnp.exp(jnp.sum(lambda_q2.astype(jnp.float32) * lambda_k2.astype(jnp.float32)))
           + lam_init)
    slopes = alibi_slopes()

    sb_fn = lambda qb, pb: stick_breaking_block(qb, pb, ks, vs, k_pos)
    df_fn = lambda qb, pb: diff_attention_block(qb, pb, kd, vd, k_pos, lam, slopes)
    if past is None:
        o_sb = sweep_query_blocks(sb_fn, q_sb, q_pos)
        o_df = sweep_query_blocks(df_fn, q_df, q_pos)
    else:
        o_sb = sb_fn(q_sb, q_pos)
        o_df = df_fn(q_df, q_pos)

    o_df = rmsnorm(o_df, diff_norm_g) * (1.0 - lam_init)
    g = jax.nn.sigmoid(gates)
    g_sb, g_df = jnp.split(g, 2, axis=-1)
    merged = (g_sb * (o_sb.reshape(b, t, SB_WIDTH) @ w_branch_sb)
              + g_df * (o_df.reshape(b, t, DIFF_V_WIDTH) @ w_branch_diff))
    x = x + merged @ w_out
    h2 = rmsnorm(x, norm2_g)
    x = x + jnp.square(jax.nn.relu(h2 @ w_up)) @ w_down
    return x, new_rows


def run_trunk(x, q_pos, caches, norm1_g, w_in, lambda_q1, lambda_k1, lambda_q2, lambda_k2,
              diff_norm_g, w_branch_sb, w_branch_diff, w_out, norm2_g, w_up, w_down, final_norm_g):
    rows = []
    for l in range(DEPTH):
        past = None if caches is None else tuple(c[l] for c in caches)
        x, new_rows = hybrid_layer(x, q_pos, past, l, norm1_g[l], w_in[l], lambda_q1[l], lambda_k1[l],
                                   lambda_q2[l], lambda_k2[l], diff_norm_g[l], w_branch_sb[l],
                                   w_branch_diff[l], w_out[l], norm2_g[l], w_up[l], w_down[l])
        rows.append(new_rows)
    stacked = tuple(jnp.stack([r[i] for r in rows], axis=0) for i in range(4))
    return rmsnorm(x, final_norm_g), stacked


def setup_inputs(seed: int = 0) -> dict:
    key = jax.random.key(seed)
    ks = jax.random.split(key, 20)
    f32 = jnp.float32

    def normal(k, shape):
        return jax.random.normal(k, shape, f32)

    def weight(k, shape, fan_in):
        return normal(k, shape) * (fan_in ** -0.5)

    def gain(k, shape):
        return 1.0 + 0.05 * normal(k, shape)

    return {
        "x_prompt": normal(ks[0], (BATCH, SEQ, D_MODEL)),
        "x_sample": normal(ks[1], (DEC_BATCH, DEC_SEQ, D_MODEL)),
        "cache_sb_k": normal(ks[2], (DEPTH, DEC_BATCH, PAST_LEN, SB_HEADS, SB_HEAD_DIM)),
        "cache_sb_v": normal(ks[3], (DEPTH, DEC_BATCH, PAST_LEN, SB_HEADS, SB_HEAD_DIM)),
        "cache_diff_k": normal(ks[4], (DEPTH, DEC_BATCH, PAST_LEN, DIFF_HEADS, 2 * DIFF_QK_DIM)),
        "cache_diff_v": normal(ks[5], (DEPTH, DEC_BATCH, PAST_LEN, DIFF_HEADS, DIFF_V_DIM)),
        "norm1_g": gain(ks[6], (DEPTH, D_MODEL)),
        "w_in": weight(ks[7], (DEPTH, D_MODEL, IN_COLS), D_MODEL),
        "lambda_q1": LAMBDA_STD * normal(ks[8], (DEPTH, DIFF_QK_DIM)),
        "lambda_k1": LAMBDA_STD * normal(ks[9], (DEPTH, DIFF_QK_DIM)),
        "lambda_q2": LAMBDA_STD * normal(ks[10], (DEPTH, DIFF_QK_DIM)),
        "lambda_k2": LAMBDA_STD * normal(ks[11], (DEPTH, DIFF_QK_DIM)),
        "diff_norm_g": gain(ks[12], (DEPTH, DIFF_V_DIM)),
        "w_branch_sb": weight(ks[13], (DEPTH, SB_WIDTH, D_MODEL), SB_WIDTH),
        "w_branch_diff": weight(ks[14], (DEPTH, DIFF_V_WIDTH, D_MODEL), DIFF_V_WIDTH),
        "w_out": weight(ks[15], (DEPTH, D_MODEL, D_MODEL), D_MODEL),
        "norm2_g": gain(ks[16], (DEPTH, D_MODEL)),
        "w_up": weight(ks[17], (DEPTH, D_MODEL, D_FF), D_MODEL),
        "w_down": weight(ks[18], (DEPTH, D_FF, D_MODEL), D_FF),
        "final_norm_g": gain(ks[19], (D_MODEL,)),
    }


def reference(x_prompt, x_sample, cache_sb_k, cache_sb_v, cache_diff_k, cache_diff_v,
              norm1_g, w_in, lambda_q1, lambda_k1, lambda_q2, lambda_k2, diff_norm_g,
              w_branch_sb, w_branch_diff, w_out, norm2_g, w_up, w_down, final_norm_g):
    q_pos_prompt = jnp.arange(x_prompt.shape[1], dtype=jnp.int32)
    q_pos_sample = PAST_LEN + jnp.arange(x_sample.shape[1], dtype=jnp.int32)
    y_prompt, rows_p = run_trunk(x_prompt, q_pos_prompt, None, norm1_g, w_in, lambda_q1, lambda_k1,
                                 lambda_q2, lambda_k2, diff_norm_g, w_branch_sb, w_branch_diff, w_out,
                                 norm2_g, w_up, w_down, final_norm_g)
    y_sample, rows_s = run_trunk(x_sample, q_pos_sample,
                                 (cache_sb_k, cache_sb_v, cache_diff_k, cache_diff_v),
                                 norm1_g, w_in, lambda_q1, lambda_k1, lambda_q2, lambda_k2, diff_norm_g,
                                 w_branch_sb, w_branch_diff, w_out, norm2_g, w_up, w_down, final_norm_g)
    new_sb_k_prompt, new_sb_v_prompt, new_diff_k_prompt, new_diff_v_prompt = rows_p
    new_sb_k_sample, new_sb_v_sample, new_diff_k_sample, new_diff_v_sample = rows_s
    return (y_prompt, y_sample,
            new_sb_k_prompt, new_sb_v_prompt, new_diff_k_prompt, new_diff_v_prompt,
            new_sb_k_sample, new_sb_v_sample, new_diff_k_sample, new_diff_v_sample)
```

```python
import functools
import math

import jax
import jax.numpy as jnp
from jax import lax
from jax.experimental import pallas as pl
from jax.experimental.pallas import tpu as pltpu

D_MODEL = 2048
DEPTH = 2
PAST_LEN = 1024
CHUNK = 64
HEADS = 8
HEAD_DIM = 128
DIFF_QK_DIM = 64
WIDTH = HEADS * HEAD_DIM
D_FF = 4 * D_MODEL
IN_COLS = 6 * WIDTH + 2 * D_MODEL
EPS = 1e-6

SB_SCALE = HEAD_DIM ** -0.5
DIFF_SCALE = DIFF_QK_DIM ** -0.5
NEG_BIG = -1e30

KEY_BLOCK = 128
Q_TILE = 128
M_TILE = 512
N_TILE = 1024
F_TILE = 1024
MERGE_COL_CHUNK = 512

VMEM_LIMIT_BYTES = 56 * 1024 * 1024

F32 = jnp.float32
BF16 = jnp.bfloat16


def _params(*semantics):
    return pltpu.CompilerParams(dimension_semantics=semantics, vmem_limit_bytes=VMEM_LIMIT_BYTES)


def _dot(a, b):
    return jnp.dot(a, b, preferred_element_type=F32)


def _dot_nt(a, b):
    return lax.dot_general(a, b, (((1,), (1,)), ((), ())), preferred_element_type=F32)


def _rmsnorm_rows(x, g):
    return x * lax.rsqrt(jnp.mean(x * x, axis=-1, keepdims=True) + EPS) * g


def _inproj_kernel(x_ref, g_ref, w_ref, qsb_ref, ksb_ref, vsb_ref, qdf_ref, kdf_ref, vdf_ref,
                   gates_ref, h_ref):
    j = pl.program_id(1)

    @pl.when(j == 0)
    def _():
        h_ref[...] = _rmsnorm_rows(x_ref[...], g_ref[...]).astype(BF16)

    y = _dot(h_ref[...], w_ref[...])
    for i, ref in enumerate((qsb_ref, ksb_ref, vsb_ref, qdf_ref, kdf_ref, vdf_ref)):
        @pl.when(j == i)
        def _(ref=ref):
            ref[...] = y.astype(ref.dtype)

    @pl.when(j >= 6)
    def _():
        gates_ref[...] = y.astype(BF16)


def _in_projection(x, g, w_in):
    m = x.shape[0]
    tm = min(M_TILE, m)
    group = lambda dt: jax.ShapeDtypeStruct((m, WIDTH), dt)
    group_spec = pl.BlockSpec((tm, N_TILE), lambda i, j: (i, 0))
    return pl.pallas_call(
        _inproj_kernel,
        grid=(m // tm, IN_COLS // N_TILE),
        in_specs=[
            pl.BlockSpec((tm, D_MODEL), lambda i, j: (i, 0)),
            pl.BlockSpec((1, D_MODEL), lambda i, j: (0, 0)),
            pl.BlockSpec((D_MODEL, N_TILE), lambda i, j: (0, j)),
        ],
        out_specs=[group_spec] * 6 + [
            pl.BlockSpec((tm, N_TILE), lambda i, j: (i, jnp.maximum(j - 6, 0)))],
        out_shape=[group(BF16), group(F32), group(F32), group(BF16), group(F32), group(F32),
                   jax.ShapeDtypeStruct((m, 2 * D_MODEL), BF16)],
        scratch_shapes=[pltpu.VMEM((tm, D_MODEL), BF16)],
        compiler_params=_params("arbitrary", "arbitrary"),
        name="in_projection",
    )(x, g, w_in)


def _cumsum_matrix(n_keys, n_bcast):
    s = jnp.arange(n_keys)[:, None]
    j = jnp.arange(n_keys)[None, :]
    tri = (s > j).astype(BF16)
    return jnp.concatenate([jnp.ones((n_keys, n_bcast), BF16), tri], axis=1)


def _sb_block(q, k, v, cm, carry, visible):
    c, acc = carry
    n_bcast = c.shape[1]
    z = _dot_nt(q, k) * SB_SCALE
    softplus = jnp.maximum(z, 0.0) + jnp.log1p(jnp.exp(-jnp.abs(z)))
    log_1m = -softplus
    log_beta = z - softplus
    if visible is not None:
        log_1m = jnp.where(visible, log_1m, 0.0)
    hi = log_1m.astype(BF16)
    lo = (log_1m - hi.astype(F32)).astype(BF16)
    sums = _dot(hi, cm) + _dot(lo, cm)
    w = jnp.exp(log_beta + sums[:, n_bcast:] + c[:, :z.shape[1]])
    if visible is not None:
        w = jnp.where(visible, w, 0.0)
    acc = acc + _dot(w.astype(BF16), v)
    return c + sums[:, :n_bcast], acc


def _sb_prompt_kernel(q_ref, k_ref, v_ref, cm_ref, o_ref, kb_ref, vb_ref):
    qi = pl.program_id(2)

    @pl.when(qi == 0)
    def _():
        kb_ref[...] = k_ref[...].astype(BF16)
        vb_ref[...] = v_ref[...].astype(BF16)

    q = q_ref[...]
    cm = cm_ref[...]
    row = lax.broadcasted_iota(jnp.int32, (Q_TILE, KEY_BLOCK), 0)
    col = lax.broadcasted_iota(jnp.int32, (Q_TILE, KEY_BLOCK), 1)

    def block(kb, carry, visible):
        off = pl.multiple_of(kb * KEY_BLOCK, KEY_BLOCK)
        return _sb_block(q, kb_ref[pl.ds(off, KEY_BLOCK), :], vb_ref[pl.ds(off, KEY_BLOCK), :],
                         cm, carry, visible)

    carry = (jnp.zeros((Q_TILE, KEY_BLOCK), F32), jnp.zeros((Q_TILE, HEAD_DIM), F32))
    carry = block(qi, carry, col < row)
    carry = lax.fori_loop(0, qi, lambda i, cr: block(qi - 1 - i, cr, None), carry)
    o_ref[...] = carry[1].astype(o_ref.dtype)


def _sb_attention_prompt(q, k, v, batch):
    assert Q_TILE == KEY_BLOCK
    m = q.shape[0]
    t = m // batch
    q, k, v = (a.reshape(batch, t, WIDTH) for a in (q, k, v))
    kv_spec = pl.BlockSpec((None, t, HEAD_DIM), lambda b, h, i: (b, 0, h))
    q_spec = pl.BlockSpec((None, Q_TILE, HEAD_DIM), lambda b, h, i: (b, i, h))
    out = pl.pallas_call(
        _sb_prompt_kernel,
        grid=(batch, HEADS, t // Q_TILE),
        in_specs=[q_spec, kv_spec, kv_spec,
                  pl.BlockSpec((KEY_BLOCK, 2 * KEY_BLOCK), lambda b, h, i: (0, 0))],
        out_specs=q_spec,
        out_shape=jax.ShapeDtypeStruct((batch, t, WIDTH), BF16),
        scratch_shapes=[pltpu.VMEM((t, HEAD_DIM), BF16), pltpu.VMEM((t, HEAD_DIM), BF16)],
        compiler_params=_params("arbitrary", "arbitrary", "arbitrary"),
        name="sb_attention_prompt",
    )(q, k, v, _cumsum_matrix(KEY_BLOCK, KEY_BLOCK))
    return out.reshape(m, WIDTH)


def _sb_sample_kernel(q_ref, k_ref, v_ref, pk_ref, pv_ref, cm_new_ref, cm_ref, o_ref, kb_ref, vb_ref):
    kb_ref[...] = pk_ref[...].astype(BF16)
    vb_ref[...] = pv_ref[...].astype(BF16)
    q = q_ref[...]
    t = q.shape[0]
    row = lax.broadcasted_iota(jnp.int32, (t, t), 0)
    col = lax.broadcasted_iota(jnp.int32, (t, t), 1)
    carry = (jnp.zeros((t, KEY_BLOCK), F32), jnp.zeros((t, HEAD_DIM), F32))
    carry = _sb_block(q, k_ref[...].astype(BF16), v_ref[...].astype(BF16), cm_new_ref[...], carry,
                      col < row)
    cm = cm_ref[...]
    n_past = PAST_LEN // KEY_BLOCK

    def past_block(i, cr):
        off = pl.multiple_of((n_past - 1 - i) * KEY_BLOCK, KEY_BLOCK)
        return _sb_block(q, kb_ref[pl.ds(off, KEY_BLOCK), :], vb_ref[pl.ds(off, KEY_BLOCK), :],
                         cm, cr, None)

    carry = lax.fori_loop(0, n_past, past_block, carry)
    o_ref[...] = carry[1].astype(o_ref.dtype)


def _sb_attention_sample(q, k, v, cache_k, cache_v, layer, batch):
    m = q.shape[0]
    t = m // batch
    q, k, v = (a.reshape(batch, t, WIDTH) for a in (q, k, v))
    new_spec = pl.BlockSpec((None, t, HEAD_DIM), lambda b, h: (b, 0, h))
    past_spec = pl.BlockSpec((None, None, PAST_LEN, HEAD_DIM), lambda b, h: (layer, b, 0, h))
    out = pl.pallas_call(
        _sb_sample_kernel,
        grid=(batch, HEADS),
        in_specs=[new_spec, new_spec, new_spec, past_spec, past_spec,
                  pl.BlockSpec((t, KEY_BLOCK + t), lambda b, h: (0, 0)),
                  pl.BlockSpec((KEY_BLOCK, 2 * KEY_BLOCK), lambda b, h: (0, 0))],
        out_specs=new_spec,
        out_shape=jax.ShapeDtypeStruct((batch, t, WIDTH), BF16),
        scratch_shapes=[pltpu.VMEM((PAST_LEN, HEAD_DIM), BF16), pltpu.VMEM((PAST_LEN, HEAD_DIM), BF16)],
        compiler_params=_params("arbitrary", "arbitrary"),
        name="sb_attention_sample",
    )(q, k, v, cache_k, cache_v, _cumsum_matrix(t, KEY_BLOCK), _cumsum_matrix(KEY_BLOCK, KEY_BLOCK))
    return out.reshape(m, WIDTH)


def _lambda_value(lamq_ref, lamk_ref, lam_init):
    e = jnp.exp(jnp.sum(lamq_ref[...] * lamk_ref[...], axis=-1, keepdims=True))
    first = lax.broadcasted_iota(jnp.int32, e.shape, 0) == 0
    return jnp.sum(jnp.where(first, e, -e), axis=0, keepdims=True) + lam_init


def _split_halves(q):
    lane = lax.broadcasted_iota(jnp.int32, q.shape, 1)
    zero = jnp.zeros_like(q)
    return jnp.where(lane < DIFF_QK_DIM, q, zero), jnp.where(lane >= DIFF_QK_DIM, q, zero)


def _softmax_step(s, v, state):
    m, l, acc = state
    m_new = jnp.maximum(m, jnp.max(s, axis=-1, keepdims=True))
    alpha = jnp.exp(m - m_new)
    p = jnp.exp(s - m_new)
    l = alpha * l + jnp.sum(p, axis=-1, keepdims=True)
    acc = alpha * acc + _dot(p.astype(BF16), v)
    return m_new, l, acc


def _diff_block(q1, q2, k, v, bias, state):
    s1 = _dot_nt(q1, k) * DIFF_SCALE + bias
    s2 = _dot_nt(q2, k) * DIFF_SCALE + bias
    return _softmax_step(s1, v, state[0]), _softmax_step(s2, v, state[1])


def _diff_init(t):
    one = (jnp.full((t, 1), NEG_BIG, F32), jnp.zeros((t, 1), F32), jnp.zeros((t, HEAD_DIM), F32))
    return one, one


def _diff_finish(state, lam, g, lam_init):
    (_, l1, a1), (_, l2, a2) = state
    o = a1 / l1 - lam * (a2 / l2)
    return _rmsnorm_rows(o, g) * (1.0 - lam_init)


def _diff_prompt_kernel(slopes_ref, q_ref, k_ref, v_ref, lamq_ref, lamk_ref, g_ref, o_ref, kb_ref, vb_ref, *,
                        lam_init):
    h = pl.program_id(1)
    qi = pl.program_id(2)

    @pl.when(qi == 0)
    def _():
        kb_ref[...] = k_ref[...].astype(BF16)
        vb_ref[...] = v_ref[...].astype(BF16)

    q1, q2 = _split_halves(q_ref[...])
    slope = slopes_ref[h]
    row = lax.broadcasted_iota(jnp.int32, (Q_TILE, KEY_BLOCK), 0)
    col = lax.broadcasted_iota(jnp.int32, (Q_TILE, KEY_BLOCK), 1)
    q_pos = qi * Q_TILE + row

    def block(kb, state):
        off = pl.multiple_of(kb * KEY_BLOCK, KEY_BLOCK)
        k_pos = off + col
        dist = jnp.abs(q_pos - k_pos).astype(F32)
        visible = (k_pos // CHUNK) <= (q_pos // CHUNK)
        bias = jnp.where(visible, -slope * dist, NEG_BIG)
        return _diff_block(q1, q2, kb_ref[pl.ds(off, KEY_BLOCK), :], vb_ref[pl.ds(off, KEY_BLOCK), :],
                           bias, state)

    state = lax.fori_loop(0, qi + 1, block, _diff_init(Q_TILE))
    o_ref[...] = _diff_finish(state, _lambda_value(lamq_ref, lamk_ref, lam_init), g_ref[...], lam_init).astype(o_ref.dtype)


def _diff_attention_prompt(q, k, v, lam_params, g, slopes, lam_init, batch):
    assert Q_TILE == KEY_BLOCK and Q_TILE % CHUNK == 0
    m = q.shape[0]
    t = m // batch
    q, k, v = (a.reshape(batch, t, WIDTH) for a in (q, k, v))
    kv_spec = pl.BlockSpec((None, t, HEAD_DIM), lambda b, h, i: (b, 0, h))
    q_spec = pl.BlockSpec((None, Q_TILE, HEAD_DIM), lambda b, h, i: (b, i, h))
    out = pl.pallas_call(
        functools.partial(_diff_prompt_kernel, lam_init=lam_init),
        grid=(batch, HEADS, t // Q_TILE),
        in_specs=[pl.BlockSpec(memory_space=pltpu.SMEM), q_spec, kv_spec, kv_spec,
                  pl.BlockSpec((2, DIFF_QK_DIM), lambda b, h, i: (0, 0)),
                  pl.BlockSpec((2, DIFF_QK_DIM), lambda b, h, i: (0, 0)),
                  pl.BlockSpec((1, HEAD_DIM), lambda b, h, i: (0, 0))],
        out_specs=q_spec,
        out_shape=jax.ShapeDtypeStruct((batch, t, WIDTH), BF16),
        scratch_shapes=[pltpu.VMEM((t, HEAD_DIM), BF16), pltpu.VMEM((t, HEAD_DIM), BF16)],
        compiler_params=_params("arbitrary", "arbitrary", "arbitrary"),
        name="diff_attention_prompt",
    )(slopes, q, k, v, *lam_params, g)
    return out.reshape(m, WIDTH)


def _diff_sample_kernel(slopes_ref, q_ref, k_ref, v_ref, pk_ref, pv_ref, lamq_ref, lamk_ref, g_ref, o_ref,
                        kb_ref, vb_ref, *, lam_init):
    h = pl.program_id(1)
    kb_ref[...] = pk_ref[...].astype(BF16)
    vb_ref[...] = pv_ref[...].astype(BF16)
    q1, q2 = _split_halves(q_ref[...])
    t = q1.shape[0]
    slope = slopes_ref[h]
    row = lax.broadcasted_iota(jnp.int32, (t, KEY_BLOCK), 0)
    col = lax.broadcasted_iota(jnp.int32, (t, KEY_BLOCK), 1)
    q_pos = PAST_LEN + row

    def past_block(kb, state):
        off = pl.multiple_of(kb * KEY_BLOCK, KEY_BLOCK)
        bias = -slope * (q_pos - (off + col)).astype(F32)
        return _diff_block(q1, q2, kb_ref[pl.ds(off, KEY_BLOCK), :], vb_ref[pl.ds(off, KEY_BLOCK), :],
                           bias, state)

    state = lax.fori_loop(0, PAST_LEN // KEY_BLOCK, past_block, _diff_init(t))
    new_row = lax.broadcasted_iota(jnp.int32, (t, t), 0)
    new_col = lax.broadcasted_iota(jnp.int32, (t, t), 1)
    bias_new = -slope * jnp.abs(new_row - new_col).astype(F32)
    state = _diff_block(q1, q2, k_ref[...].astype(BF16), v_ref[...].astype(BF16), bias_new, state)
    o_ref[...] = _diff_finish(state, _lambda_value(lamq_ref, lamk_ref, lam_init), g_ref[...], lam_init).astype(o_ref.dtype)


def _diff_attention_sample(q, k, v, cache_k, cache_v, layer, lam_params, g, slopes, lam_init, batch):
    m = q.shape[0]
    t = m // batch
    assert t == CHUNK and PAST_LEN % CHUNK == 0
    q, k, v = (a.reshape(batch, t, WIDTH) for a in (q, k, v))
    new_spec = pl.BlockSpec((None, t, HEAD_DIM), lambda b, h: (b, 0, h))
    past_spec = pl.BlockSpec((None, None, PAST_LEN, HEAD_DIM), lambda b, h: (layer, b, 0, h))
    out = pl.pallas_call(
        functools.partial(_diff_sample_kernel, lam_init=lam_init),
        grid=(batch, HEADS),
        in_specs=[pl.BlockSpec(memory_space=pltpu.SMEM), new_spec, new_spec, new_spec, past_spec, past_spec,
                  pl.BlockSpec((2, DIFF_QK_DIM), lambda b, h: (0, 0)),
                  pl.BlockSpec((2, DIFF_QK_DIM), lambda b, h: (0, 0)),
                  pl.BlockSpec((1, HEAD_DIM), lambda b, h: (0, 0))],
        out_specs=new_spec,
        out_shape=jax.ShapeDtypeStruct((batch, t, WIDTH), BF16),
        scratch_shapes=[pltpu.VMEM((PAST_LEN, HEAD_DIM), BF16), pltpu.VMEM((PAST_LEN, HEAD_DIM), BF16)],
        compiler_params=_params("arbitrary", "arbitrary"),
        name="diff_attention_sample",
    )(slopes, q, k, v, cache_k, cache_v, *lam_params, g)
    return out.reshape(m, WIDTH)


def _sigmoid(x):
    return 1.0 / (1.0 + jnp.exp(-x))


def _merge_kernel(osb_ref, odf_ref, gates_ref, x_ref, wsb_ref, wdf_ref, wout_ref, o_ref, merged_ref):
    osb = osb_ref[...]
    odf = odf_ref[...]
    for c in range(D_MODEL // MERGE_COL_CHUNK):
        cols = slice(c * MERGE_COL_CHUNK, (c + 1) * MERGE_COL_CHUNK)
        gcols = slice(D_MODEL + c * MERGE_COL_CHUNK, D_MODEL + (c + 1) * MERGE_COL_CHUNK)
        g_sb = _sigmoid(gates_ref[:, cols].astype(F32))
        g_df = _sigmoid(gates_ref[:, gcols].astype(F32))
        merged = g_sb * _dot(osb, wsb_ref[:, cols]) + g_df * _dot(odf, wdf_ref[:, cols])
        merged_ref[:, cols] = merged.astype(BF16)
    o_ref[...] = x_ref[...] + _dot(merged_ref[...], wout_ref[...])


def _merge_project(o_sb, o_df, gates, x, w_sb, w_df, w_out):
    m = x.shape[0]
    tm = min(M_TILE, m)
    row_spec = lambda width: pl.BlockSpec((tm, width), lambda i: (i, 0))
    whole = lambda a: pl.BlockSpec(a.shape, lambda i: (0, 0), pipeline_mode=pl.Buffered(1))
    return pl.pallas_call(
        _merge_kernel,
        grid=(m // tm,),
        in_specs=[row_spec(WIDTH), row_spec(WIDTH), row_spec(2 * D_MODEL), row_spec(D_MODEL),
                  whole(w_sb), whole(w_df), whole(w_out)],
        out_specs=row_spec(D_MODEL),
        out_shape=jax.ShapeDtypeStruct((m, D_MODEL), F32),
        scratch_shapes=[pltpu.VMEM((tm, D_MODEL), BF16)],
        compiler_params=_params("arbitrary"),
        name="merge_project",
    )(o_sb, o_df, gates, x, w_sb, w_df, w_out)


def _mlp_kernel(x_ref, g_ref, wup_ref, wdown_ref, gf_ref, o_ref, h_ref, acc_ref, *, final_norm):
    f = pl.program_id(1)

    @pl.when(f == 0)
    def _():
        h_ref[...] = _rmsnorm_rows(x_ref[...], g_ref[...]).astype(BF16)
        acc_ref[...] = jnp.zeros_like(acc_ref)

    u = jnp.maximum(_dot(h_ref[...], wup_ref[...]), 0.0)
    acc_ref[...] += _dot((u * u).astype(BF16), wdown_ref[...])

    @pl.when(f == pl.num_programs(1) - 1)
    def _():
        y = x_ref[...] + acc_ref[...]
        if final_norm:
            y = _rmsnorm_rows(y, gf_ref[...])
        o_ref[...] = y


def _mlp(x, g, w_up, w_down, g_final, final_norm):
    m = x.shape[0]
    tm = min(M_TILE, m)
    vec_spec = pl.BlockSpec((1, D_MODEL), lambda i, f: (0, 0))
    return pl.pallas_call(
        functools.partial(_mlp_kernel, final_norm=final_norm),
        grid=(m // tm, D_FF // F_TILE),
        in_specs=[pl.BlockSpec((tm, D_MODEL), lambda i, f: (i, 0)), vec_spec,
                  pl.BlockSpec((D_MODEL, F_TILE), lambda i, f: (0, f)),
                  pl.BlockSpec((F_TILE, D_MODEL), lambda i, f: (f, 0)), vec_spec],
        out_specs=pl.BlockSpec((tm, D_MODEL), lambda i, f: (i, 0)),
        out_shape=jax.ShapeDtypeStruct((m, D_MODEL), F32),
        scratch_shapes=[pltpu.VMEM((tm, D_MODEL), BF16), pltpu.VMEM((tm, D_MODEL), F32)],
        compiler_params=_params("arbitrary", "arbitrary"),
        name="mlp",
    )(x, g, w_up, w_down, g_final)


def _trunk(x, caches, weights, slopes):
    batch, t, _ = x.shape
    x = x.reshape(batch * t, D_MODEL)
    rows = []
    for layer in range(DEPTH):
        w = {name: value[layer] for name, value in weights.items() if name != "final_norm_g"}
        lam_init = 0.8 - 0.6 * math.exp(-0.3 * layer)
        q_sb, k_sb, v_sb, q_df, k_df, v_df, gates = _in_projection(x, w["norm1_g"][None], w["w_in"])
        lam_params = (jnp.stack([w["lambda_q1"], w["lambda_q2"]]), jnp.stack([w["lambda_k1"], w["lambda_k2"]]))
        g_diff = w["diff_norm_g"][None]
        if caches is None:
            o_sb = _sb_attention_prompt(q_sb, k_sb, v_sb, batch)
            o_df = _diff_attention_prompt(q_df, k_df, v_df, lam_params, g_diff, slopes, lam_init, batch)
        else:
            c_sb_k, c_sb_v, c_df_k, c_df_v = caches
            o_sb = _sb_attention_sample(q_sb, k_sb, v_sb, c_sb_k, c_sb_v, layer, batch)
            o_df = _diff_attention_sample(q_df, k_df, v_df, c_df_k, c_df_v, layer, lam_params, g_diff,
                                          slopes, lam_init, batch)
        x = _merge_project(o_sb, o_df, gates, x, w["w_branch_sb"], w["w_branch_diff"], w["w_out"])
        x = _mlp(x, w["norm2_g"][None], w["w_up"], w["w_down"], weights["final_norm_g"][None],
                 final_norm=(layer == DEPTH - 1))
        rows.append((k_sb, v_sb, k_df, v_df))
    stacked = tuple(jnp.stack([r[i] for r in rows]).reshape(DEPTH, batch, t, HEADS, HEAD_DIM)
                    for i in range(4))
    return x.reshape(batch, t, D_MODEL), stacked


def kernel(x_prompt, x_sample, cache_sb_k, cache_sb_v, cache_diff_k, cache_diff_v, norm1_g, w_in, lambda_q1, lambda_k1, lambda_q2, lambda_k2, diff_norm_g, w_branch_sb, w_branch_diff, w_out, norm2_g, w_up, w_down, final_norm_g):
    weights = dict(norm1_g=norm1_g, w_in=w_in.astype(BF16), lambda_q1=lambda_q1, lambda_k1=lambda_k1,
                   lambda_q2=lambda_q2, lambda_k2=lambda_k2, diff_norm_g=diff_norm_g,
                   w_branch_sb=w_branch_sb.astype(BF16), w_branch_diff=w_branch_diff.astype(BF16),
                   w_out=w_out.astype(BF16), norm2_g=norm2_g, w_up=w_up.astype(BF16),
                   w_down=w_down.astype(BF16), final_norm_g=final_norm_g)
    slopes = jnp.exp2(-8.0 * jnp.arange(1, HEADS + 1, dtype=F32) / HEADS)
    caches = tuple(c.reshape(DEPTH, c.shape[1], PAST_LEN, WIDTH)
                   for c in (cache_sb_k, cache_sb_v, cache_diff_k, cache_diff_v))
    y_prompt, rows_p = _trunk(x_prompt, None, weights, slopes)
    y_sample, rows_s = _trunk(x_sample, caches, weights, slopes)
    return (y_prompt, y_sample) + rows_p + rows_s
```

```python
import functools
import math

import jax
import jax.numpy as jnp
from jax import lax
from jax.experimental import pallas as pl
from jax.experimental.pallas import tpu as pltpu

D_MODEL = 2048
DEPTH = 2
PAST_LEN = 1024
CHUNK = 64
CHUNK_SHIFT = CHUNK.bit_length() - 1
assert CHUNK == 1 << CHUNK_SHIFT
HEADS = 8
HEAD_DIM = 128
DIFF_QK_DIM = 64
WIDTH = HEADS * HEAD_DIM
D_FF = 4 * D_MODEL
IN_COLS = 6 * WIDTH + 2 * D_MODEL
EPS = 1e-6

LOG2E = math.log2(math.e)
SB_Q_SCALE = HEAD_DIM ** -0.5 * LOG2E
DIFF_Q_SCALE = DIFF_QK_DIM ** -0.5 * LOG2E
NEG_BIG = -1e30
SOFTPLUS_CLAMP = 30.0

LANES = 128
SB_Q_TILE = 512
SB_KEY_STEP = 256
DIFF_Q_TILE = 512
DIFF_KEY_STEP = 256
M_TILE = 512
N_TILE = 1024
F_TILE = 1024
MERGE_COL_CHUNK = 512

VMEM_LIMIT_BYTES = 56 * 1024 * 1024

F32 = jnp.float32
BF16 = jnp.bfloat16


def _params(*semantics):
    return pltpu.CompilerParams(dimension_semantics=semantics, vmem_limit_bytes=VMEM_LIMIT_BYTES)


def _dot(a, b):
    return jnp.dot(a, b, preferred_element_type=F32)


def _dot_nt(a, b):
    return lax.dot_general(a, b, (((1,), (1,)), ((), ())), preferred_element_type=F32)


def _rmsnorm_rows(x, g):
    return x * lax.rsqrt(jnp.mean(x * x, axis=-1, keepdims=True) + EPS) * g


def _inproj_kernel(x_ref, g_ref, w_ref, qsb_ref, ksb_ref, vsb_ref, qdf_ref, kdf_ref, vdf_ref,
                   gates_ref, h_ref):
    j = pl.program_id(1)

    @pl.when(j == 0)
    def _():
        h_ref[...] = _rmsnorm_rows(x_ref[...], g_ref[...]).astype(BF16)

    y = _dot(h_ref[...], w_ref[...])
    groups = ((qsb_ref, SB_Q_SCALE), (ksb_ref, None), (vsb_ref, None),
              (qdf_ref, DIFF_Q_SCALE), (kdf_ref, None), (vdf_ref, None))
    for i, (ref, scale) in enumerate(groups):
        @pl.when(j == i)
        def _(ref=ref, scale=scale):
            ref[...] = (y if scale is None else y * scale).astype(ref.dtype)

    @pl.when(j >= len(groups))
    def _():
        gates_ref[...] = y.astype(BF16)


def _in_projection(x, g, w_in):
    m = x.shape[0]
    tm = min(M_TILE, m)
    group = lambda dt: jax.ShapeDtypeStruct((m, WIDTH), dt)
    group_spec = pl.BlockSpec((tm, N_TILE), lambda i, j: (i, 0))
    return pl.pallas_call(
        _inproj_kernel,
        grid=(m // tm, IN_COLS // N_TILE),
        in_specs=[
            pl.BlockSpec((tm, D_MODEL), lambda i, j: (i, 0)),
            pl.BlockSpec((1, D_MODEL), lambda i, j: (0, 0)),
            pl.BlockSpec((D_MODEL, N_TILE), lambda i, j: (0, j)),
        ],
        out_specs=[group_spec] * 6 + [
            pl.BlockSpec((tm, N_TILE), lambda i, j: (i, jnp.maximum(j - 6, 0)))],
        out_shape=[group(BF16), group(F32), group(F32), group(BF16), group(F32), group(F32),
                   jax.ShapeDtypeStruct((m, 2 * D_MODEL), BF16)],
        scratch_shapes=[pltpu.VMEM((tm, D_MODEL), BF16)],
        compiler_params=_params("arbitrary", "arbitrary"),
        name="in_projection",
    )(x, g, w_in)


def _prefix_matrix():
    s = jnp.arange(LANES)[:, None]
    j = jnp.arange(LANES)[None, :]
    half = jnp.concatenate([jnp.ones((LANES, LANES), BF16), (s > j).astype(BF16)], axis=1)
    return jnp.concatenate([half, half], axis=0)


def _softplus2(x):
    return jnp.maximum(x, jnp.log2(1.0 + jnp.exp2(jnp.minimum(x, SOFTPLUS_CLAMP))))


def _sb_step(q, k, v, pm, carry, visible):
    c, acc = carry
    z = _dot_nt(q, k)
    sp = _softplus2(z)
    log_beta = z - sp
    if visible is not None:
        sp = jnp.where(visible, sp, 0.0)
    hi = sp.astype(BF16)
    lo = (sp - hi.astype(F32)).astype(BF16)
    args = []
    for s in reversed(range(z.shape[1] // LANES)):
        cols = slice(s * LANES, (s + 1) * LANES)
        sums = _dot(jnp.concatenate([hi[:, cols], lo[:, cols]], axis=1), pm)
        args.insert(0, log_beta[:, cols] - sums[:, LANES:] - c)
        c = c + sums[:, :LANES]
    w = jnp.exp2(jnp.concatenate(args, axis=1))
    if visible is not None:
        w = jnp.where(visible, w, 0.0)
    return c, acc + _dot(w.astype(BF16), v)


def _sb_prompt_kernel(q_ref, k_ref, v_ref, pm_ref, o_ref, kb_ref, vb_ref):
    qi = pl.program_id(2)

    @pl.when(qi == 0)
    def _():
        kb_ref[...] = k_ref[...].astype(BF16)
        vb_ref[...] = v_ref[...].astype(BF16)

    q = q_ref[...]
    pm = pm_ref[...]
    row = lax.broadcasted_iota(jnp.int32, (SB_Q_TILE, SB_KEY_STEP), 0)
    col = lax.broadcasted_iota(jnp.int32, (SB_Q_TILE, SB_KEY_STEP), 1)
    steps_per_tile = SB_Q_TILE // SB_KEY_STEP

    def step(kb, carry, visible):
        off = pl.multiple_of(kb * SB_KEY_STEP, SB_KEY_STEP)
        return _sb_step(q, kb_ref[pl.ds(off, SB_KEY_STEP), :], vb_ref[pl.ds(off, SB_KEY_STEP), :],
                        pm, carry, visible)

    carry = (jnp.zeros((SB_Q_TILE, LANES), F32), jnp.zeros((SB_Q_TILE, HEAD_DIM), F32))
    for p in reversed(range(steps_per_tile)):
        carry = step(qi * steps_per_tile + p, carry, col + p * SB_KEY_STEP < row)
    n_full = qi * steps_per_tile
    carry = lax.fori_loop(0, n_full, lambda i, cr: step(n_full - 1 - i, cr, None), carry)
    o_ref[...] = carry[1].astype(o_ref.dtype)


def _sb_attention_prompt(q, k, v, batch):
    m = q.shape[0]
    t = m // batch
    q, k, v = (a.reshape(batch, t, WIDTH) for a in (q, k, v))
    kv_spec = pl.BlockSpec((None, t, HEAD_DIM), lambda b, h, i: (b, 0, h))
    q_spec = pl.BlockSpec((None, SB_Q_TILE, HEAD_DIM), lambda b, h, i: (b, i, h))
    out = pl.pallas_call(
        _sb_prompt_kernel,
        grid=(batch, HEADS, t // SB_Q_TILE),
        in_specs=[q_spec, kv_spec, kv_spec,
                  pl.BlockSpec((2 * LANES, 2 * LANES), lambda b, h, i: (0, 0))],
        out_specs=q_spec,
        out_shape=jax.ShapeDtypeStruct((batch, t, WIDTH), BF16),
        scratch_shapes=[pltpu.VMEM((t, HEAD_DIM), BF16), pltpu.VMEM((t, HEAD_DIM), BF16)],
        compiler_params=_params("arbitrary", "arbitrary", "arbitrary"),
        name="sb_attention_prompt",
    )(q, k, v, _prefix_matrix())
    return out.reshape(m, WIDTH)


def _sb_sample_kernel(q_ref, k_ref, v_ref, pk_ref, pv_ref, pm_ref, o_ref):
    pm = pm_ref[...]
    t = q_ref.shape[0]
    row = lax.broadcasted_iota(jnp.int32, (t, LANES), 0)
    col = lax.broadcasted_iota(jnp.int32, (t, LANES), 1)
    pad = jnp.zeros((LANES - t, HEAD_DIM), BF16)
    for h in range(HEADS):
        cols = slice(h * HEAD_DIM, (h + 1) * HEAD_DIM)
        q = q_ref[:, cols]
        carry = (jnp.zeros((t, LANES), F32), jnp.zeros((t, HEAD_DIM), F32))
        k_new = jnp.concatenate([k_ref[:, cols].astype(BF16), pad], axis=0)
        v_new = jnp.concatenate([v_ref[:, cols].astype(BF16), pad], axis=0)
        carry = _sb_step(q, k_new, v_new, pm, carry, col < row)
        for s in reversed(range(PAST_LEN // SB_KEY_STEP)):
            keys = slice(s * SB_KEY_STEP, (s + 1) * SB_KEY_STEP)
            carry = _sb_step(q, pk_ref[keys, cols].astype(BF16), pv_ref[keys, cols].astype(BF16),
                             pm, carry, None)
        o_ref[:, cols] = carry[1].astype(o_ref.dtype)


def _sb_attention_sample(q, k, v, cache_k, cache_v, layer, batch):
    m = q.shape[0]
    t = m // batch
    assert t <= LANES
    q, k, v = (a.reshape(batch, t, WIDTH) for a in (q, k, v))
    new_spec = pl.BlockSpec((None, t, WIDTH), lambda b: (b, 0, 0))
    past_spec = pl.BlockSpec((None, None, PAST_LEN, WIDTH), lambda b: (layer, b, 0, 0))
    out = pl.pallas_call(
        _sb_sample_kernel,
        grid=(batch,),
        in_specs=[new_spec, new_spec, new_spec, past_spec, past_spec,
                  pl.BlockSpec((2 * LANES, 2 * LANES), lambda b: (0, 0))],
        out_specs=new_spec,
        out_shape=jax.ShapeDtypeStruct((batch, t, WIDTH), BF16),
        compiler_params=_params("arbitrary"),
        name="sb_attention_sample",
    )(q, k, v, cache_k, cache_v, _prefix_matrix())
    return out.reshape(m, WIDTH)


def _lambda_value(lamq_ref, lamk_ref, lam_init):
    e = jnp.exp(jnp.sum(lamq_ref[...] * lamk_ref[...], axis=-1, keepdims=True))
    first = lax.broadcasted_iota(jnp.int32, e.shape, 0) == 0
    return jnp.sum(jnp.where(first, e, -e), axis=0, keepdims=True) + lam_init


def _split_halves(q):
    lane = lax.broadcasted_iota(jnp.int32, q.shape, 1)
    zero = jnp.zeros_like(q)
    return jnp.where(lane < DIFF_QK_DIM, q, zero), jnp.where(lane >= DIFF_QK_DIM, q, zero)


def _diff_finish(o1, o2, lam, g, lam_init):
    return _rmsnorm_rows(o1 - lam * o2, g) * (1.0 - lam_init)


def _online_softmax_step(qz, k, v_ones, bias, state):
    m, acc = state
    s = _dot_nt(qz, k) + bias
    m_new = jnp.maximum(m, jnp.max(s, axis=-1, keepdims=True))
    p = jnp.exp2(s - m_new)
    return m_new, jnp.exp2(m - m_new) * acc + _dot(p.astype(BF16), v_ones)


def _diff_prompt_kernel(slopes_ref, q_ref, k_ref, v_ref, lamq_ref, lamk_ref, g_ref, o_ref, kb_ref, vb_ref, *,
                        lam_init):
    h = pl.program_id(1)
    qi = pl.program_id(2)

    @pl.when(qi == 0)
    def _():
        kb_ref[...] = k_ref[...].astype(BF16)
        vb_ref[:, :HEAD_DIM] = v_ref[...].astype(BF16)
        vb_ref[:, HEAD_DIM:] = jnp.ones((vb_ref.shape[0], HEAD_DIM), BF16)

    q1, q2 = _split_halves(q_ref[...])
    slope = slopes_ref[h]
    row = lax.broadcasted_iota(jnp.int32, (DIFF_Q_TILE, DIFF_KEY_STEP), 0)
    col = lax.broadcasted_iota(jnp.int32, (DIFF_Q_TILE, DIFF_KEY_STEP), 1)
    key_col = lax.broadcasted_iota(jnp.int32, (1, DIFF_KEY_STEP), 1)
    steps_per_tile = DIFF_Q_TILE // DIFF_KEY_STEP

    def step(kb, bias, state):
        off = pl.multiple_of(kb * DIFF_KEY_STEP, DIFF_KEY_STEP)
        k = kb_ref[pl.ds(off, DIFF_KEY_STEP), :]
        v_ones = vb_ref[pl.ds(off, DIFF_KEY_STEP), :]
        return tuple(_online_softmax_step(qz, k, v_ones, bias, st) for qz, st in zip((q1, q2), state))

    def full_step(kb, state):
        bias = slope * (kb * DIFF_KEY_STEP + key_col - qi * DIFF_Q_TILE).astype(F32)
        return step(kb, bias, state)

    one = (jnp.full((DIFF_Q_TILE, 1), NEG_BIG, F32), jnp.zeros((DIFF_Q_TILE, 2 * HEAD_DIM), F32))
    state = (one, one)
    for p in range(steps_per_tile):
        k_in_tile = col + p * DIFF_KEY_STEP
        visible = (k_in_tile >> CHUNK_SHIFT) <= (row >> CHUNK_SHIFT)
        bias = jnp.where(visible, slope * (row - jnp.abs(row - k_in_tile)).astype(F32), NEG_BIG)
        state = step(qi * steps_per_tile + p, bias, state)
    state = lax.fori_loop(0, qi * steps_per_tile, full_step, state)
    (_, a1), (_, a2) = state
    o = _diff_finish(a1[:, :HEAD_DIM] / a1[:, HEAD_DIM:], a2[:, :HEAD_DIM] / a2[:, HEAD_DIM:],
                     _lambda_value(lamq_ref, lamk_ref, lam_init), g_ref[...], lam_init)
    o_ref[...] = o.astype(o_ref.dtype)


def _diff_attention_prompt(q, k, v, lam_params, g, slopes, lam_init, batch):
    assert DIFF_KEY_STEP % CHUNK == 0
    m = q.shape[0]
    t = m // batch
    q, k, v = (a.reshape(batch, t, WIDTH) for a in (q, k, v))
    kv_spec = pl.BlockSpec((None, t, HEAD_DIM), lambda b, h, i: (b, 0, h))
    q_spec = pl.BlockSpec((None, DIFF_Q_TILE, HEAD_DIM), lambda b, h, i: (b, i, h))
    lam_spec = pl.BlockSpec((2, DIFF_QK_DIM), lambda b, h, i: (0, 0))
    out = pl.pallas_call(
        functools.partial(_diff_prompt_kernel, lam_init=lam_init),
        grid=(batch, HEADS, t // DIFF_Q_TILE),
        in_specs=[pl.BlockSpec(memory_space=pltpu.SMEM), q_spec, kv_spec, kv_spec, lam_spec, lam_spec,
                  pl.BlockSpec((1, HEAD_DIM), lambda b, h, i: (0, 0))],
        out_specs=q_spec,
        out_shape=jax.ShapeDtypeStruct((batch, t, WIDTH), BF16),
        scratch_shapes=[pltpu.VMEM((t, HEAD_DIM), BF16), pltpu.VMEM((t, 2 * HEAD_DIM), BF16)],
        compiler_params=_params("arbitrary", "arbitrary", "arbitrary"),
        name="diff_attention_prompt",
    )(slopes, q, k, v, *lam_params, g)
    return out.reshape(m, WIDTH)


def _diff_sample_kernel(slopes_ref, q_ref, k_ref, v_ref, pk_ref, pv_ref, lamq_ref, lamk_ref, g_ref, o_ref, *,
                        lam_init):
    t = q_ref.shape[0]
    lam = _lambda_value(lamq_ref, lamk_ref, lam_init)
    g = g_ref[...]
    past_col = lax.broadcasted_iota(jnp.int32, (1, PAST_LEN), 1)
    row = lax.broadcasted_iota(jnp.int32, (t, t), 0)
    col = lax.broadcasted_iota(jnp.int32, (t, t), 1)
    past_dist = (past_col - PAST_LEN).astype(F32)
    new_dist = (row - jnp.abs(row - col)).astype(F32)
    for h in range(HEADS):
        cols = slice(h * HEAD_DIM, (h + 1) * HEAD_DIM)
        slope = slopes_ref[h]
        k_past = pk_ref[:, cols].astype(BF16)
        v_past = pv_ref[:, cols].astype(BF16)
        k_new = k_ref[:, cols].astype(BF16)
        v_new = v_ref[:, cols].astype(BF16)
        outs = []
        for qz in _split_halves(q_ref[:, cols]):
            s_past = _dot_nt(qz, k_past) + slope * past_dist
            s_new = _dot_nt(qz, k_new) + slope * new_dist
            m = jnp.maximum(jnp.max(s_past, axis=-1, keepdims=True), jnp.max(s_new, axis=-1, keepdims=True))
            p_past = jnp.exp2(s_past - m)
            p_new = jnp.exp2(s_new - m)
            total = jnp.sum(p_past, axis=-1, keepdims=True) + jnp.sum(p_new, axis=-1, keepdims=True)
            outs.append((_dot(p_past.astype(BF16), v_past) + _dot(p_new.astype(BF16), v_new)) / total)
        o_ref[:, cols] = _diff_finish(outs[0], outs[1], lam, g, lam_init).astype(o_ref.dtype)


def _diff_attention_sample(q, k, v, cache_k, cache_v, layer, lam_params, g, slopes, lam_init, batch):
    m = q.shape[0]
    t = m // batch
    assert t == CHUNK and PAST_LEN % CHUNK == 0
    q, k, v = (a.reshape(batch, t, WIDTH) for a in (q, k, v))
    new_spec = pl.BlockSpec((None, t, WIDTH), lambda b: (b, 0, 0))
    past_spec = pl.BlockSpec((None, None, PAST_LEN, WIDTH), lambda b: (layer, b, 0, 0))
    lam_spec = pl.BlockSpec((2, DIFF_QK_DIM), lambda b: (0, 0))
    out = pl.pallas_call(
        functools.partial(_diff_sample_kernel, lam_init=lam_init),
        grid=(batch,),
        in_specs=[pl.BlockSpec(memory_space=pltpu.SMEM), new_spec, new_spec, new_spec, past_spec, past_spec,
                  lam_spec, lam_spec, pl.BlockSpec((1, HEAD_DIM), lambda b: (0, 0))],
        out_specs=new_spec,
        out_shape=jax.ShapeDtypeStruct((batch, t, WIDTH), BF16),
        compiler_params=_params("arbitrary"),
        name="diff_attention_sample",
    )(slopes, q, k, v, cache_k, cache_v, *lam_params, g)
    return out.reshape(m, WIDTH)


def _sigmoid(x):
    return 1.0 / (1.0 + jnp.exp(-x))


def _merge_kernel(osb_ref, odf_ref, gates_ref, x_ref, wsb_ref, wdf_ref, wout_ref, o_ref, merged_ref):
    osb = osb_ref[...]
    odf = odf_ref[...]
    for c in range(D_MODEL // MERGE_COL_CHUNK):
        cols = slice(c * MERGE_COL_CHUNK, (c + 1) * MERGE_COL_CHUNK)
        gcols = slice(D_MODEL + c * MERGE_COL_CHUNK, D_MODEL + (c + 1) * MERGE_COL_CHUNK)
        g_sb = _sigmoid(gates_ref[:, cols].astype(F32))
        g_df = _sigmoid(gates_ref[:, gcols].astype(F32))
        merged = g_sb * _dot(osb, wsb_ref[:, cols]) + g_df * _dot(odf, wdf_ref[:, cols])
        merged_ref[:, cols] = merged.astype(BF16)
    o_ref[...] = x_ref[...] + _dot(merged_ref[...], wout_ref[...])


def _merge_project(o_sb, o_df, gates, x, w_sb, w_df, w_out):
    m = x.shape[0]
    tm = min(M_TILE, m)
    row_spec = lambda width: pl.BlockSpec((tm, width), lambda i: (i, 0))
    whole = lambda a: pl.BlockSpec(a.shape, lambda i: (0, 0), pipeline_mode=pl.Buffered(1))
    return pl.pallas_call(
        _merge_kernel,
        grid=(m // tm,),
        in_specs=[row_spec(WIDTH), row_spec(WIDTH), row_spec(2 * D_MODEL), row_spec(D_MODEL),
                  whole(w_sb), whole(w_df), whole(w_out)],
        out_specs=row_spec(D_MODEL),
        out_shape=jax.ShapeDtypeStruct((m, D_MODEL), F32),
        scratch_shapes=[pltpu.VMEM((tm, D_MODEL), BF16)],
        compiler_params=_params("arbitrary"),
        name="merge_project",
    )(o_sb, o_df, gates, x, w_sb, w_df, w_out)


def _mlp_kernel(x_ref, g_ref, wup_ref, wdown_ref, gf_ref, o_ref, h_ref, acc_ref, *, final_norm):
    f = pl.program_id(1)

    @pl.when(f == 0)
    def _():
        h_ref[...] = _rmsnorm_rows(x_ref[...], g_ref[...]).astype(BF16)
        acc_ref[...] = jnp.zeros_like(acc_ref)

    u = jnp.maximum(_dot(h_ref[...], wup_ref[...]), 0.0)
    acc_ref[...] += _dot((u * u).astype(BF16), wdown_ref[...])

    @pl.when(f == pl.num_programs(1) - 1)
    def _():
        y = x_ref[...] + acc_ref[...]
        if final_norm:
            y = _rmsnorm_rows(y, gf_ref[...])
        o_ref[...] = y


def _mlp(x, g, w_up, w_down, g_final, final_norm):
    m = x.shape[0]
    tm = min(M_TILE, m)
    vec_spec = pl.BlockSpec((1, D_MODEL), lambda i, f: (0, 0))
    return pl.pallas_call(
        functools.partial(_mlp_kernel, final_norm=final_norm),
        grid=(m // tm, D_FF // F_TILE),
        in_specs=[pl.BlockSpec((tm, D_MODEL), lambda i, f: (i, 0)), vec_spec,
                  pl.BlockSpec((D_MODEL, F_TILE), lambda i, f: (0, f)),
                  pl.BlockSpec((F_TILE, D_MODEL), lambda i, f: (f, 0)), vec_spec],
        out_specs=pl.BlockSpec((tm, D_MODEL), lambda i, f: (i, 0)),
        out_shape=jax.ShapeDtypeStruct((m, D_MODEL), F32),
        scratch_shapes=[pltpu.VMEM((tm, D_MODEL), BF16), pltpu.VMEM((tm, D_MODEL), F32)],
        compiler_params=_params("arbitrary", "arbitrary"),
        name="mlp",
    )(x, g, w_up, w_down, g_final)


def _trunk(x, caches, weights, slopes):
    batch, t, _ = x.shape
    x = x.reshape(batch * t, D_MODEL)
    rows = []
    for layer in range(DEPTH):
        w = {name: value[layer] for name, value in weights.items() if name != "final_norm_g"}
        lam_init = 0.8 - 0.6 * math.exp(-0.3 * layer)
        q_sb, k_sb, v_sb, q_df, k_df, v_df, gates = _in_projection(x, w["norm1_g"][None], w["w_in"])
        lam_params = (jnp.stack([w["lambda_q1"], w["lambda_q2"]]), jnp.stack([w["lambda_k1"], w["lambda_k2"]]))
        g_diff = w["diff_norm_g"][None]
        if caches is None:
            o_sb = _sb_attention_prompt(q_sb, k_sb, v_sb, batch)
            o_df = _diff_attention_prompt(q_df, k_df, v_df, lam_params, g_diff, slopes, lam_init, batch)
        else:
            c_sb_k, c_sb_v, c_df_k, c_df_v = caches
            o_sb = _sb_attention_sample(q_sb, k_sb, v_sb, c_sb_k, c_sb_v, layer, batch)
            o_df = _diff_attention_sample(q_df, k_df, v_df, c_df_k, c_df_v, layer, lam_params, g_diff,
                                          slopes, lam_init, batch)
        x = _merge_project(o_sb, o_df, gates, x, w["w_branch_sb"], w["w_branch_diff"], w["w_out"])
        x = _mlp(x, w["norm2_g"][None], w["w_up"], w["w_down"], weights["final_norm_g"][None],
                 final_norm=(layer == DEPTH - 1))
        rows.append((k_sb, v_sb, k_df, v_df))
    stacked = tuple(jnp.stack([r[i] for r in rows]).reshape(DEPTH, batch, t, HEADS, HEAD_DIM)
                    for i in range(4))
    return x.reshape(batch, t, D_MODEL), stacked


def kernel(x_prompt, x_sample, cache_sb_k, cache_sb_v, cache_diff_k, cache_diff_v, norm1_g, w_in, lambda_q1, lambda_k1, lambda_q2, lambda_k2, diff_norm_g, w_branch_sb, w_branch_diff, w_out, norm2_g, w_up, w_down, final_norm_g):
    weights = dict(norm1_g=norm1_g, w_in=w_in.astype(BF16), lambda_q1=lambda_q1, lambda_k1=lambda_k1,
                   lambda_q2=lambda_q2, lambda_k2=lambda_k2, diff_norm_g=diff_norm_g,
                   w_branch_sb=w_branch_sb.astype(BF16), w_branch_diff=w_branch_diff.astype(BF16),
                   w_out=w_out.astype(BF16), norm2_g=norm2_g, w_up=w_up.astype(BF16),
                   w_down=w_down.astype(BF16), final_norm_g=final_norm_g)
    slopes = jnp.exp2(-8.0 * jnp.arange(1, HEADS + 1, dtype=F32) / HEADS) * LOG2E
    caches = tuple(c.reshape(DEPTH, c.shape[1], PAST_LEN, WIDTH)
                   for c in (cache_sb_k, cache_sb_v, cache_diff_k, cache_diff_v))
    y_prompt, rows_p = _trunk(x_prompt, None, weights, slopes)
    y_sample, rows_s = _trunk(x_sample, caches, weights, slopes)
    return (y_prompt, y_sample) + rows_p + rows_s
```

```python
import functools
import math

import jax
import jax.numpy as jnp
from jax import lax
from jax.experimental import pallas as pl
from jax.experimental.pallas import tpu as pltpu

D_MODEL = 2048
DEPTH = 2
PAST_LEN = 1024
CHUNK = 64
CHUNK_SHIFT = CHUNK.bit_length() - 1
assert CHUNK == 1 << CHUNK_SHIFT
HEADS = 8
HEAD_DIM = 128
DIFF_QK_DIM = 64
WIDTH = HEADS * HEAD_DIM
D_FF = 4 * D_MODEL
IN_COLS = 6 * WIDTH + 2 * D_MODEL
EPS = 1e-6

LOG2E = math.log2(math.e)
SB_Q_SCALE = HEAD_DIM ** -0.5 * LOG2E
DIFF_Q_SCALE = DIFF_QK_DIM ** -0.5 * LOG2E
NEG_BIG = -1e30
SOFTPLUS_CLAMP = 30.0

LANES = 128
SB_Q_TILE = 512
SB_KEY_STEP = 256
DIFF_Q_TILE = 512
DIFF_KEY_STEP = 256
M_TILE = 512
N_TILE = 1024
F_TILE = 1024
MERGE_COL_CHUNK = 512

VMEM_LIMIT_BYTES = 56 * 1024 * 1024

F32 = jnp.float32
BF16 = jnp.bfloat16


def _params(*semantics):
    return pltpu.CompilerParams(dimension_semantics=semantics, vmem_limit_bytes=VMEM_LIMIT_BYTES)


def _dot(a, b):
    return jnp.dot(a, b, preferred_element_type=F32)


def _dot_nt(a, b):
    return lax.dot_general(a, b, (((1,), (1,)), ((), ())), preferred_element_type=F32)


def _rmsnorm_rows(x, g):
    return x * lax.rsqrt(jnp.mean(x * x, axis=-1, keepdims=True) + EPS) * g


def _inproj_kernel(x_ref, g_ref, w_ref, ksb_in, vsb_in, kdf_in, vdf_in,
                   qsb_ref, ksb_ref, vsb_ref, qdf_ref, kdf_ref, vdf_ref, gates_ref, h_ref):
    del ksb_in, vsb_in, kdf_in, vdf_in
    j = pl.program_id(1)

    @pl.when(j == 0)
    def _():
        h_ref[...] = _rmsnorm_rows(x_ref[...], g_ref[...]).astype(BF16)

    y = _dot(h_ref[...], w_ref[...])
    groups = ((qsb_ref, SB_Q_SCALE), (ksb_ref, None), (vsb_ref, None),
              (qdf_ref, DIFF_Q_SCALE), (kdf_ref, None), (vdf_ref, None))
    for i, (ref, scale) in enumerate(groups):
        @pl.when(j == i)
        def _(ref=ref, scale=scale):
            ref[...] = (y if scale is None else y * scale).astype(ref.dtype)

    @pl.when(j >= len(groups))
    def _():
        gates_ref[...] = y.astype(BF16)


def _in_projection(x, g, w_in, layer, kv_rows):
    m = x.shape[0]
    tm = min(M_TILE, m)
    q_shape = jax.ShapeDtypeStruct((m, WIDTH), BF16)
    kv_shape = jax.ShapeDtypeStruct((DEPTH, m, WIDTH), F32)
    q_spec = pl.BlockSpec((tm, N_TILE), lambda i, j: (i, 0))
    kv_spec = pl.BlockSpec((None, tm, N_TILE), lambda i, j: (layer, i, 0))
    return pl.pallas_call(
        _inproj_kernel,
        grid=(m // tm, IN_COLS // N_TILE),
        in_specs=[
            pl.BlockSpec((tm, D_MODEL), lambda i, j: (i, 0)),
            pl.BlockSpec((1, D_MODEL), lambda i, j: (0, 0)),
            pl.BlockSpec((D_MODEL, N_TILE), lambda i, j: (0, j)),
        ] + [pl.BlockSpec(memory_space=pl.ANY)] * 4,
        out_specs=[q_spec, kv_spec, kv_spec, q_spec, kv_spec, kv_spec,
                   pl.BlockSpec((tm, N_TILE), lambda i, j: (i, jnp.maximum(j - 6, 0)))],
        out_shape=[q_shape, kv_shape, kv_shape, q_shape, kv_shape, kv_shape,
                   jax.ShapeDtypeStruct((m, 2 * D_MODEL), BF16)],
        input_output_aliases={3: 1, 4: 2, 5: 4, 6: 5},
        scratch_shapes=[pltpu.VMEM((tm, D_MODEL), BF16)],
        compiler_params=_params("arbitrary", "arbitrary"),
        name="in_projection",
    )(x, g, w_in, *kv_rows)


def _prefix_matrix():
    s = jnp.arange(LANES)[:, None]
    j = jnp.arange(LANES)[None, :]
    half = jnp.concatenate([jnp.ones((LANES, LANES), BF16), (s > j).astype(BF16)], axis=1)
    return jnp.concatenate([half, half], axis=0)


def _softplus2(x):
    return jnp.maximum(x, jnp.log2(1.0 + jnp.exp2(jnp.minimum(x, SOFTPLUS_CLAMP))))


def _sb_step(q, k, v, pm, carry, visible):
    c, acc = carry
    z = _dot_nt(q, k)
    sp = _softplus2(z)
    log_beta = z - sp
    if visible is not None:
        sp = jnp.where(visible, sp, 0.0)
    hi = sp.astype(BF16)
    lo = (sp - hi.astype(F32)).astype(BF16)
    args = []
    for s in reversed(range(z.shape[1] // LANES)):
        cols = slice(s * LANES, (s + 1) * LANES)
        sums = _dot(jnp.concatenate([hi[:, cols], lo[:, cols]], axis=1), pm)
        args.insert(0, log_beta[:, cols] - sums[:, LANES:] - c)
        c = c + sums[:, :LANES]
    w = jnp.exp2(jnp.concatenate(args, axis=1))
    if visible is not None:
        w = jnp.where(visible, w, 0.0)
    return c, acc + _dot(w.astype(BF16), v)


def _sb_prompt_kernel(q_ref, k_ref, v_ref, pm_ref, o_ref, kb_ref, vb_ref):
    qi = pl.program_id(2)

    @pl.when(qi == 0)
    def _():
        kb_ref[...] = k_ref[...].astype(BF16)
        vb_ref[...] = v_ref[...].astype(BF16)

    q = q_ref[...]
    pm = pm_ref[...]
    row = lax.broadcasted_iota(jnp.int32, (SB_Q_TILE, SB_KEY_STEP), 0)
    col = lax.broadcasted_iota(jnp.int32, (SB_Q_TILE, SB_KEY_STEP), 1)
    steps_per_tile = SB_Q_TILE // SB_KEY_STEP

    def step(kb, carry, visible):
        off = pl.multiple_of(kb * SB_KEY_STEP, SB_KEY_STEP)
        return _sb_step(q, kb_ref[pl.ds(off, SB_KEY_STEP), :], vb_ref[pl.ds(off, SB_KEY_STEP), :],
                        pm, carry, visible)

    carry = (jnp.zeros((SB_Q_TILE, LANES), F32), jnp.zeros((SB_Q_TILE, HEAD_DIM), F32))
    for p in reversed(range(steps_per_tile)):
        carry = step(qi * steps_per_tile + p, carry, col + p * SB_KEY_STEP < row)
    n_full = qi * steps_per_tile
    carry = lax.fori_loop(0, n_full, lambda i, cr: step(n_full - 1 - i, cr, None), carry)
    o_ref[...] = carry[1].astype(o_ref.dtype)


def _sb_attention_prompt(q, k, v, layer, batch):
    m = q.shape[0]
    t = m // batch
    q = q.reshape(batch, t, WIDTH)
    k, v = (a.reshape(DEPTH, batch, t, WIDTH) for a in (k, v))
    kv_spec = pl.BlockSpec((None, None, t, HEAD_DIM), lambda b, h, i: (layer, b, 0, h))
    q_spec = pl.BlockSpec((None, SB_Q_TILE, HEAD_DIM), lambda b, h, i: (b, i, h))
    out = pl.pallas_call(
        _sb_prompt_kernel,
        grid=(batch, HEADS, t // SB_Q_TILE),
        in_specs=[q_spec, kv_spec, kv_spec,
                  pl.BlockSpec((2 * LANES, 2 * LANES), lambda b, h, i: (0, 0))],
        out_specs=q_spec,
        out_shape=jax.ShapeDtypeStruct((batch, t, WIDTH), BF16),
        scratch_shapes=[pltpu.VMEM((t, HEAD_DIM), BF16), pltpu.VMEM((t, HEAD_DIM), BF16)],
        compiler_params=_params("arbitrary", "arbitrary", "arbitrary"),
        name="sb_attention_prompt",
    )(q, k, v, _prefix_matrix())
    return out.reshape(m, WIDTH)


def _sb_sample_kernel(q_ref, k_ref, v_ref, pk_ref, pv_ref, pm_ref, o_ref):
    pm = pm_ref[...]
    t = q_ref.shape[0]
    row = lax.broadcasted_iota(jnp.int32, (t, LANES), 0)
    col = lax.broadcasted_iota(jnp.int32, (t, LANES), 1)
    pad = jnp.zeros((LANES - t, HEAD_DIM), BF16)
    for h in range(HEADS):
        cols = slice(h * HEAD_DIM, (h + 1) * HEAD_DIM)
        q = q_ref[:, cols]
        carry = (jnp.zeros((t, LANES), F32), jnp.zeros((t, HEAD_DIM), F32))
        k_new = jnp.concatenate([k_ref[:, cols].astype(BF16), pad], axis=0)
        v_new = jnp.concatenate([v_ref[:, cols].astype(BF16), pad], axis=0)
        carry = _sb_step(q, k_new, v_new, pm, carry, col < row)
        for s in reversed(range(PAST_LEN // SB_KEY_STEP)):
            keys = slice(s * SB_KEY_STEP, (s + 1) * SB_KEY_STEP)
            carry = _sb_step(q, pk_ref[keys, cols].astype(BF16), pv_ref[keys, cols].astype(BF16),
                             pm, carry, None)
        o_ref[:, cols] = carry[1].astype(o_ref.dtype)


def _sb_attention_sample(q, k, v, cache_k, cache_v, layer, batch):
    m = q.shape[0]
    t = m // batch
    assert t <= LANES
    q = q.reshape(batch, t, WIDTH)
    k, v = (a.reshape(DEPTH, batch, t, WIDTH) for a in (k, v))
    new_spec = pl.BlockSpec((None, t, WIDTH), lambda b: (b, 0, 0))
    new_kv_spec = pl.BlockSpec((None, None, t, WIDTH), lambda b: (layer, b, 0, 0))
    past_spec = pl.BlockSpec((None, None, PAST_LEN, WIDTH), lambda b: (layer, b, 0, 0))
    out = pl.pallas_call(
        _sb_sample_kernel,
        grid=(batch,),
        in_specs=[new_spec, new_kv_spec, new_kv_spec, past_spec, past_spec,
                  pl.BlockSpec((2 * LANES, 2 * LANES), lambda b: (0, 0))],
        out_specs=new_spec,
        out_shape=jax.ShapeDtypeStruct((batch, t, WIDTH), BF16),
        compiler_params=_params("arbitrary"),
        name="sb_attention_sample",
    )(q, k, v, cache_k, cache_v, _prefix_matrix())
    return out.reshape(m, WIDTH)


def _lambda_value(lamq_ref, lamk_ref, lam_init):
    e = jnp.exp(jnp.sum(lamq_ref[...] * lamk_ref[...], axis=-1, keepdims=True))
    first = lax.broadcasted_iota(jnp.int32, e.shape, 0) == 0
    return jnp.sum(jnp.where(first, e, -e), axis=0, keepdims=True) + lam_init


def _split_halves(q):
    lane = lax.broadcasted_iota(jnp.int32, q.shape, 1)
    zero = jnp.zeros_like(q)
    return jnp.where(lane < DIFF_QK_DIM, q, zero), jnp.where(lane >= DIFF_QK_DIM, q, zero)


def _diff_finish(o1, o2, lam, g, lam_init):
    return _rmsnorm_rows(o1 - lam * o2, g) * (1.0 - lam_init)


def _online_softmax_step(qz, k, v_ones, bias, state):
    m, acc = state
    s = _dot_nt(qz, k) + bias
    m_new = jnp.maximum(m, jnp.max(s, axis=-1, keepdims=True))
    p = jnp.exp2(s - m_new)
    return m_new, jnp.exp2(m - m_new) * acc + _dot(p.astype(BF16), v_ones)


def _diff_prompt_kernel(slopes_ref, q_ref, k_ref, v_ref, lamq_ref, lamk_ref, g_ref, o_ref, kb_ref, vb_ref, *,
                        lam_init):
    h = pl.program_id(1)
    qi = pl.program_id(2)

    @pl.when(qi == 0)
    def _():
        kb_ref[...] = k_ref[...].astype(BF16)
        vb_ref[:, :HEAD_DIM] = v_ref[...].astype(BF16)
        vb_ref[:, HEAD_DIM:] = jnp.ones((vb_ref.shape[0], HEAD_DIM), BF16)

    q1, q2 = _split_halves(q_ref[...])
    slope = slopes_ref[h]
    row = lax.broadcasted_iota(jnp.int32, (DIFF_Q_TILE, DIFF_KEY_STEP), 0)
    col = lax.broadcasted_iota(jnp.int32, (DIFF_Q_TILE, DIFF_KEY_STEP), 1)
    key_col = lax.broadcasted_iota(jnp.int32, (1, DIFF_KEY_STEP), 1)
    steps_per_tile = DIFF_Q_TILE // DIFF_KEY_STEP

    def step(kb, bias, state):
        off = pl.multiple_of(kb * DIFF_KEY_STEP, DIFF_KEY_STEP)
        k = kb_ref[pl.ds(off, DIFF_KEY_STEP), :]
        v_ones = vb_ref[pl.ds(off, DIFF_KEY_STEP), :]
        return tuple(_online_softmax_step(qz, k, v_ones, bias, st) for qz, st in zip((q1, q2), state))

    def full_step(kb, state):
        bias = slope * (kb * DIFF_KEY_STEP + key_col - qi * DIFF_Q_TILE).astype(F32)
        return step(kb, bias, state)

    one = (jnp.full((DIFF_Q_TILE, 1), NEG_BIG, F32), jnp.zeros((DIFF_Q_TILE, 2 * HEAD_DIM), F32))
    state = (one, one)
    for p in range(steps_per_tile):
        k_in_tile = col + p * DIFF_KEY_STEP
        visible = (k_in_tile >> CHUNK_SHIFT) <= (row >> CHUNK_SHIFT)
        bias = jnp.where(visible, slope * (row - jnp.abs(row - k_in_tile)).astype(F32), NEG_BIG)
        state = step(qi * steps_per_tile + p, bias, state)
    state = lax.fori_loop(0, qi * steps_per_tile, full_step, state)
    (_, a1), (_, a2) = state
    o = _diff_finish(a1[:, :HEAD_DIM] / a1[:, HEAD_DIM:], a2[:, :HEAD_DIM] / a2[:, HEAD_DIM:],
                     _lambda_value(lamq_ref, lamk_ref, lam_init), g_ref[...], lam_init)
    o_ref[...] = o.astype(o_ref.dtype)


def _diff_attention_prompt(q, k, v, layer, lam_params, g, slopes, lam_init, batch):
    assert DIFF_KEY_STEP % CHUNK == 0
    m = q.shape[0]
    t = m // batch
    q = q.reshape(batch, t, WIDTH)
    k, v = (a.reshape(DEPTH, batch, t, WIDTH) for a in (k, v))
    kv_spec = pl.BlockSpec((None, None, t, HEAD_DIM), lambda b, h, i: (layer, b, 0, h))
    q_spec = pl.BlockSpec((None, DIFF_Q_TILE, HEAD_DIM), lambda b, h, i: (b, i, h))
    lam_spec = pl.BlockSpec((2, DIFF_QK_DIM), lambda b, h, i: (0, 0))
    out = pl.pallas_call(
        functools.partial(_diff_prompt_kernel, lam_init=lam_init),
        grid=(batch, HEADS, t // DIFF_Q_TILE),
        in_specs=[pl.BlockSpec(memory_space=pltpu.SMEM), q_spec, kv_spec, kv_spec, lam_spec, lam_spec,
                  pl.BlockSpec((1, HEAD_DIM), lambda b, h, i: (0, 0))],
        out_specs=q_spec,
        out_shape=jax.ShapeDtypeStruct((batch, t, WIDTH), BF16),
        scratch_shapes=[pltpu.VMEM((t, HEAD_DIM), BF16), pltpu.VMEM((t, 2 * HEAD_DIM), BF16)],
        compiler_params=_params("arbitrary", "arbitrary", "arbitrary"),
        name="diff_attention_prompt",
    )(slopes, q, k, v, *lam_params, g)
    return out.reshape(m, WIDTH)


def _diff_sample_kernel(slopes_ref, q_ref, k_ref, v_ref, pk_ref, pv_ref, lamq_ref, lamk_ref, g_ref, o_ref, *,
                        lam_init):
    t = q_ref.shape[0]
    lam = _lambda_value(lamq_ref, lamk_ref, lam_init)
    g = g_ref[...]
    past_col = lax.broadcasted_iota(jnp.int32, (1, PAST_LEN), 1)
    row = lax.broadcasted_iota(jnp.int32, (t, t), 0)
    col = lax.broadcasted_iota(jnp.int32, (t, t), 1)
    past_dist = (past_col - PAST_LEN).astype(F32)
    new_dist = (row - jnp.abs(row - col)).astype(F32)
    for h in range(HEADS):
        cols = slice(h * HEAD_DIM, (h + 1) * HEAD_DIM)
        slope = slopes_ref[h]
        k_past = pk_ref[:, cols].astype(BF16)
        v_past = pv_ref[:, cols].astype(BF16)
        k_new = k_ref[:, cols].astype(BF16)
        v_new = v_ref[:, cols].astype(BF16)
        outs = []
        for qz in _split_halves(q_ref[:, cols]):
            s_past = _dot_nt(qz, k_past) + slope * past_dist
            s_new = _dot_nt(qz, k_new) + slope * new_dist
            m = jnp.maximum(jnp.max(s_past, axis=-1, keepdims=True), jnp.max(s_new, axis=-1, keepdims=True))
            p_past = jnp.exp2(s_past - m)
            p_new = jnp.exp2(s_new - m)
            total = jnp.sum(p_past, axis=-1, keepdims=True) + jnp.sum(p_new, axis=-1, keepdims=True)
            outs.append((_dot(p_past.astype(BF16), v_past) + _dot(p_new.astype(BF16), v_new)) / total)
        o_ref[:, cols] = _diff_finish(outs[0], outs[1], lam, g, lam_init).astype(o_ref.dtype)


def _diff_attention_sample(q, k, v, cache_k, cache_v, layer, lam_params, g, slopes, lam_init, batch):
    m = q.shape[0]
    t = m // batch
    assert t == CHUNK and PAST_LEN % CHUNK == 0
    q = q.reshape(batch, t, WIDTH)
    k, v = (a.reshape(DEPTH, batch, t, WIDTH) for a in (k, v))
    new_spec = pl.BlockSpec((None, t, WIDTH), lambda b: (b, 0, 0))
    new_kv_spec = pl.BlockSpec((None, None, t, WIDTH), lambda b: (layer, b, 0, 0))
    past_spec = pl.BlockSpec((None, None, PAST_LEN, WIDTH), lambda b: (layer, b, 0, 0))
    lam_spec = pl.BlockSpec((2, DIFF_QK_DIM), lambda b: (0, 0))
    out = pl.pallas_call(
        functools.partial(_diff_sample_kernel, lam_init=lam_init),
        grid=(batch,),
        in_specs=[pl.BlockSpec(memory_space=pltpu.SMEM), new_spec, new_kv_spec, new_kv_spec, past_spec, past_spec,
                  lam_spec, lam_spec, pl.BlockSpec((1, HEAD_DIM), lambda b: (0, 0))],
        out_specs=new_spec,
        out_shape=jax.ShapeDtypeStruct((batch, t, WIDTH), BF16),
        compiler_params=_params("arbitrary"),
        name="diff_attention_sample",
    )(slopes, q, k, v, cache_k, cache_v, *lam_params, g)
    return out.reshape(m, WIDTH)


def _sigmoid(x):
    return 1.0 / (1.0 + jnp.exp(-x))


def _merge_kernel(osb_ref, odf_ref, gates_ref, x_ref, wsb_ref, wdf_ref, wout_ref, o_ref, merged_ref):
    osb = osb_ref[...]
    odf = odf_ref[...]
    for c in range(D_MODEL // MERGE_COL_CHUNK):
        cols = slice(c * MERGE_COL_CHUNK, (c + 1) * MERGE_COL_CHUNK)
        gcols = slice(D_MODEL + c * MERGE_COL_CHUNK, D_MODEL + (c + 1) * MERGE_COL_CHUNK)
        g_sb = _sigmoid(gates_ref[:, cols].astype(F32))
        g_df = _sigmoid(gates_ref[:, gcols].astype(F32))
        merged = g_sb * _dot(osb, wsb_ref[:, cols]) + g_df * _dot(odf, wdf_ref[:, cols])
        merged_ref[:, cols] = merged.astype(BF16)
    o_ref[...] = x_ref[...] + _dot(merged_ref[...], wout_ref[...])


def _merge_project(o_sb, o_df, gates, x, w_sb, w_df, w_out):
    m = x.shape[0]
    tm = min(M_TILE, m)
    row_spec = lambda width: pl.BlockSpec((tm, width), lambda i: (i, 0))
    whole = lambda a: pl.BlockSpec(a.shape, lambda i: (0, 0), pipeline_mode=pl.Buffered(1))
    return pl.pallas_call(
        _merge_kernel,
        grid=(m // tm,),
        in_specs=[row_spec(WIDTH), row_spec(WIDTH), row_spec(2 * D_MODEL), row_spec(D_MODEL),
                  whole(w_sb), whole(w_df), whole(w_out)],
        out_specs=row_spec(D_MODEL),
        out_shape=jax.ShapeDtypeStruct((m, D_MODEL), F32),
        scratch_shapes=[pltpu.VMEM((tm, D_MODEL), BF16)],
        compiler_params=_params("arbitrary"),
        name="merge_project",
    )(o_sb, o_df, gates, x, w_sb, w_df, w_out)


def _mlp_kernel(x_ref, g_ref, wup_ref, wdown_ref, gf_ref, o_ref, h_ref, acc_ref, *, final_norm):
    f = pl.program_id(1)

    @pl.when(f == 0)
    def _():
        h_ref[...] = _rmsnorm_rows(x_ref[...], g_ref[...]).astype(BF16)
        acc_ref[...] = jnp.zeros_like(acc_ref)

    u = jnp.maximum(_dot(h_ref[...], wup_ref[...]), 0.0)
    acc_ref[...] += _dot((u * u).astype(BF16), wdown_ref[...])

    @pl.when(f == pl.num_programs(1) - 1)
    def _():
        y = x_ref[...] + acc_ref[...]
        if final_norm:
            y = _rmsnorm_rows(y, gf_ref[...])
        o_ref[...] = y


def _mlp(x, g, w_up, w_down, g_final, final_norm):
    m = x.shape[0]
    tm = min(M_TILE, m)
    vec_spec = pl.BlockSpec((1, D_MODEL), lambda i, f: (0, 0))
    return pl.pallas_call(
        functools.partial(_mlp_kernel, final_norm=final_norm),
        grid=(m // tm, D_FF // F_TILE),
        in_specs=[pl.BlockSpec((tm, D_MODEL), lambda i, f: (i, 0)), vec_spec,
                  pl.BlockSpec((D_MODEL, F_TILE), lambda i, f: (0, f)),
                  pl.BlockSpec((F_TILE, D_MODEL), lambda i, f: (f, 0)), vec_spec],
        out_specs=pl.BlockSpec((tm, D_MODEL), lambda i, f: (i, 0)),
        out_shape=jax.ShapeDtypeStruct((m, D_MODEL), F32),
        scratch_shapes=[pltpu.VMEM((tm, D_MODEL), BF16), pltpu.VMEM((tm, D_MODEL), F32)],
        compiler_params=_params("arbitrary", "arbitrary"),
        name="mlp",
    )(x, g, w_up, w_down, g_final)


def _trunk(x, caches, weights, slopes):
    batch, t, _ = x.shape
    x = x.reshape(batch * t, D_MODEL)
    kv_rows = tuple(jnp.zeros((DEPTH, batch * t, WIDTH), F32) for _ in range(4))
    for layer in range(DEPTH):
        w = {name: value[layer] for name, value in weights.items() if name != "final_norm_g"}
        lam_init = 0.8 - 0.6 * math.exp(-0.3 * layer)
        q_sb, k_sb, v_sb, q_df, k_df, v_df, gates = _in_projection(x, w["norm1_g"][None], w["w_in"], layer,
                                                                   kv_rows)
        kv_rows = (k_sb, v_sb, k_df, v_df)
        lam_params = (jnp.stack([w["lambda_q1"], w["lambda_q2"]]), jnp.stack([w["lambda_k1"], w["lambda_k2"]]))
        g_diff = w["diff_norm_g"][None]
        if caches is None:
            o_sb = _sb_attention_prompt(q_sb, k_sb, v_sb, layer, batch)
            o_df = _diff_attention_prompt(q_df, k_df, v_df, layer, lam_params, g_diff, slopes, lam_init, batch)
        else:
            c_sb_k, c_sb_v, c_df_k, c_df_v = caches
            o_sb = _sb_attention_sample(q_sb, k_sb, v_sb, c_sb_k, c_sb_v, layer, batch)
            o_df = _diff_attention_sample(q_df, k_df, v_df, c_df_k, c_df_v, layer, lam_params, g_diff,
                                          slopes, lam_init, batch)
        x = _merge_project(o_sb, o_df, gates, x, w["w_branch_sb"], w["w_branch_diff"], w["w_out"])
        x = _mlp(x, w["norm2_g"][None], w["w_up"], w["w_down"], weights["final_norm_g"][None],
                 final_norm=(layer == DEPTH - 1))
    stacked = tuple(r.reshape(DEPTH, batch, t, HEADS, HEAD_DIM) for r in kv_rows)
    return x.reshape(batch, t, D_MODEL), stacked


def kernel(x_prompt, x_sample, cache_sb_k, cache_sb_v, cache_diff_k, cache_diff_v, norm1_g, w_in, lambda_q1, lambda_k1, lambda_q2, lambda_k2, diff_norm_g, w_branch_sb, w_branch_diff, w_out, norm2_g, w_up, w_down, final_norm_g):
    weights = dict(norm1_g=norm1_g, w_in=w_in.astype(BF16), lambda_q1=lambda_q1, lambda_k1=lambda_k1,
                   lambda_q2=lambda_q2, lambda_k2=lambda_k2, diff_norm_g=diff_norm_g,
                   w_branch_sb=w_branch_sb.astype(BF16), w_branch_diff=w_branch_diff.astype(BF16),
                   w_out=w_out.astype(BF16), norm2_g=norm2_g, w_up=w_up.astype(BF16),
                   w_down=w_down.astype(BF16), final_norm_g=final_norm_g)
    slopes = jnp.exp2(-8.0 * jnp.arange(1, HEADS + 1, dtype=F32) / HEADS) * LOG2E
    caches = tuple(c.reshape(DEPTH, c.shape[1], PAST_LEN, WIDTH)
                   for c in (cache_sb_k, cache_sb_v, cache_diff_k, cache_diff_v))
    y_prompt, rows_p = _trunk(x_prompt, None, weights, slopes)
    y_sample, rows_s = _trunk(x_sample, caches, weights, slopes)
    return (y_prompt, y_sample) + rows_p + rows_s
```

```python
import functools
import math

import jax
import jax.numpy as jnp
from jax import lax
from jax.experimental import pallas as pl
from jax.experimental.pallas import tpu as pltpu

D_MODEL = 2048
DEPTH = 2
PAST_LEN = 1024
CHUNK = 64
CHUNK_SHIFT = CHUNK.bit_length() - 1
assert CHUNK == 1 << CHUNK_SHIFT
HEADS = 8
HEAD_DIM = 128
DIFF_QK_DIM = 64
WIDTH = HEADS * HEAD_DIM
D_FF = 4 * D_MODEL
IN_COLS = 6 * WIDTH + 2 * D_MODEL
EPS = 1e-6

LOG2E = math.log2(math.e)
SB_Q_SCALE = HEAD_DIM ** -0.5 * LOG2E
DIFF_Q_SCALE = DIFF_QK_DIM ** -0.5 * LOG2E
NEG_BIG = -1e30
SOFTPLUS_CLAMP = 30.0

LANES = 128
SB_Q_TILE = 512
SB_KEY_STEP = 512
DIFF_Q_TILE = 512
DIFF_KEY_STEP = 512
M_TILE = 512
N_TILE = 1024
F_TILE = 1024
MERGE_COL_CHUNK = 512

VMEM_LIMIT_BYTES = 56 * 1024 * 1024

F32 = jnp.float32
BF16 = jnp.bfloat16


def _params(*semantics):
    return pltpu.CompilerParams(dimension_semantics=semantics, vmem_limit_bytes=VMEM_LIMIT_BYTES)


def _dot(a, b):
    return jnp.dot(a, b, preferred_element_type=F32)


def _dot_nt(a, b):
    return lax.dot_general(a, b, (((1,), (1,)), ((), ())), preferred_element_type=F32)


def _rmsnorm_rows(x, g):
    return x * lax.rsqrt(jnp.mean(x * x, axis=-1, keepdims=True) + EPS) * g


def _inproj_kernel(x_ref, g_ref, w_ref, ksb_in, vsb_in, kdf_in, vdf_in,
                   qsb_ref, ksb_ref, vsb_ref, qdf_ref, kdf_ref, vdf_ref, gates_ref, h_ref):
    del ksb_in, vsb_in, kdf_in, vdf_in
    j = pl.program_id(1)

    @pl.when(j == 0)
    def _():
        h_ref[...] = _rmsnorm_rows(x_ref[...], g_ref[...]).astype(BF16)

    y = _dot(h_ref[...], w_ref[...])
    groups = ((qsb_ref, SB_Q_SCALE), (ksb_ref, None), (vsb_ref, None),
              (qdf_ref, DIFF_Q_SCALE), (kdf_ref, None), (vdf_ref, None))
    for i, (ref, scale) in enumerate(groups):
        @pl.when(j == i)
        def _(ref=ref, scale=scale):
            ref[...] = (y if scale is None else y * scale).astype(ref.dtype)

    @pl.when(j >= len(groups))
    def _():
        gates_ref[...] = y.astype(BF16)


def _in_projection(x, g, w_in, layer, kv_rows):
    m = x.shape[0]
    tm = min(M_TILE, m)
    q_shape = jax.ShapeDtypeStruct((m, WIDTH), BF16)
    kv_shape = jax.ShapeDtypeStruct((DEPTH, m, WIDTH), F32)
    q_spec = pl.BlockSpec((tm, N_TILE), lambda i, j: (i, 0))
    kv_spec = pl.BlockSpec((None, tm, N_TILE), lambda i, j: (layer, i, 0))
    return pl.pallas_call(
        _inproj_kernel,
        grid=(m // tm, IN_COLS // N_TILE),
        in_specs=[
            pl.BlockSpec((tm, D_MODEL), lambda i, j: (i, 0)),
            pl.BlockSpec((1, D_MODEL), lambda i, j: (0, 0)),
            pl.BlockSpec((D_MODEL, N_TILE), lambda i, j: (0, j)),
        ] + [pl.BlockSpec(memory_space=pl.ANY)] * 4,
        out_specs=[q_spec, kv_spec, kv_spec, q_spec, kv_spec, kv_spec,
                   pl.BlockSpec((tm, N_TILE), lambda i, j: (i, jnp.maximum(j - 6, 0)))],
        out_shape=[q_shape, kv_shape, kv_shape, q_shape, kv_shape, kv_shape,
                   jax.ShapeDtypeStruct((m, 2 * D_MODEL), BF16)],
        input_output_aliases={3: 1, 4: 2, 5: 4, 6: 5},
        scratch_shapes=[pltpu.VMEM((tm, D_MODEL), BF16)],
        compiler_params=_params("arbitrary", "arbitrary"),
        name="in_projection",
    )(x, g, w_in, *kv_rows)


def _prefix_matrix():
    s = jnp.arange(LANES)[:, None]
    j = jnp.arange(LANES)[None, :]
    half = jnp.concatenate([jnp.ones((LANES, LANES), BF16), (s > j).astype(BF16)], axis=1)
    return jnp.concatenate([half, half], axis=0)


def _softplus2(x):
    return jnp.maximum(x, jnp.log2(1.0 + jnp.exp2(jnp.minimum(x, SOFTPLUS_CLAMP))))


def _sb_step(q, k, v, pm, carry, visible):
    c, acc = carry
    z = _dot_nt(q, k)
    sp = _softplus2(z)
    log_beta = z - sp
    if visible is not None:
        sp = jnp.where(visible, sp, 0.0)
    hi = sp.astype(BF16)
    lo = (sp - hi.astype(F32)).astype(BF16)
    args = []
    for s in reversed(range(z.shape[1] // LANES)):
        cols = slice(s * LANES, (s + 1) * LANES)
        sums = _dot(jnp.concatenate([hi[:, cols], lo[:, cols]], axis=1), pm)
        args.insert(0, log_beta[:, cols] - sums[:, LANES:] - c)
        c = c + sums[:, :LANES]
    w = jnp.exp2(jnp.concatenate(args, axis=1))
    if visible is not None:
        w = jnp.where(visible, w, 0.0)
    return c, acc + _dot(w.astype(BF16), v)


def _sb_prompt_kernel(q_ref, k_ref, v_ref, pm_ref, o_ref, kb_ref, vb_ref):
    qi = pl.program_id(2)

    @pl.when(qi == 0)
    def _():
        kb_ref[...] = k_ref[...].astype(BF16)
        vb_ref[...] = v_ref[...].astype(BF16)

    q = q_ref[...]
    pm = pm_ref[...]
    row = lax.broadcasted_iota(jnp.int32, (SB_Q_TILE, SB_KEY_STEP), 0)
    col = lax.broadcasted_iota(jnp.int32, (SB_Q_TILE, SB_KEY_STEP), 1)
    steps_per_tile = SB_Q_TILE // SB_KEY_STEP

    def step(kb, carry, visible):
        off = pl.multiple_of(kb * SB_KEY_STEP, SB_KEY_STEP)
        return _sb_step(q, kb_ref[pl.ds(off, SB_KEY_STEP), :], vb_ref[pl.ds(off, SB_KEY_STEP), :],
                        pm, carry, visible)

    carry = (jnp.zeros((SB_Q_TILE, LANES), F32), jnp.zeros((SB_Q_TILE, HEAD_DIM), F32))
    for p in reversed(range(steps_per_tile)):
        carry = step(qi * steps_per_tile + p, carry, col + p * SB_KEY_STEP < row)
    n_full = qi * steps_per_tile
    carry = lax.fori_loop(0, n_full, lambda i, cr: step(n_full - 1 - i, cr, None), carry)
    o_ref[...] = carry[1].astype(o_ref.dtype)


def _sb_attention_prompt(q, k, v, layer, batch):
    m = q.shape[0]
    t = m // batch
    q = q.reshape(batch, t, WIDTH)
    k, v = (a.reshape(DEPTH, batch, t, WIDTH) for a in (k, v))
    kv_spec = pl.BlockSpec((None, None, t, HEAD_DIM), lambda b, h, i: (layer, b, 0, h))
    q_spec = pl.BlockSpec((None, SB_Q_TILE, HEAD_DIM), lambda b, h, i: (b, i, h))
    out = pl.pallas_call(
        _sb_prompt_kernel,
        grid=(batch, HEADS, t // SB_Q_TILE),
        in_specs=[q_spec, kv_spec, kv_spec,
                  pl.BlockSpec((2 * LANES, 2 * LANES), lambda b, h, i: (0, 0))],
        out_specs=q_spec,
        out_shape=jax.ShapeDtypeStruct((batch, t, WIDTH), BF16),
        scratch_shapes=[pltpu.VMEM((t, HEAD_DIM), BF16), pltpu.VMEM((t, HEAD_DIM), BF16)],
        compiler_params=_params("arbitrary", "arbitrary", "arbitrary"),
        name="sb_attention_prompt",
    )(q, k, v, _prefix_matrix())
    return out.reshape(m, WIDTH)


def _sb_sample_kernel(q_ref, k_ref, v_ref, pk_ref, pv_ref, pm_ref, o_ref):
    pm = pm_ref[...]
    t = q_ref.shape[0]
    row = lax.broadcasted_iota(jnp.int32, (t, LANES), 0)
    col = lax.broadcasted_iota(jnp.int32, (t, LANES), 1)
    pad = jnp.zeros((LANES - t, HEAD_DIM), BF16)
    for h in range(HEADS):
        cols = slice(h * HEAD_DIM, (h + 1) * HEAD_DIM)
        q = q_ref[:, cols]
        carry = (jnp.zeros((t, LANES), F32), jnp.zeros((t, HEAD_DIM), F32))
        k_new = jnp.concatenate([k_ref[:, cols].astype(BF16), pad], axis=0)
        v_new = jnp.concatenate([v_ref[:, cols].astype(BF16), pad], axis=0)
        carry = _sb_step(q, k_new, v_new, pm, carry, col < row)
        for s in reversed(range(PAST_LEN // SB_KEY_STEP)):
            keys = slice(s * SB_KEY_STEP, (s + 1) * SB_KEY_STEP)
            carry = _sb_step(q, pk_ref[keys, cols].astype(BF16), pv_ref[keys, cols].astype(BF16),
                             pm, carry, None)
        o_ref[:, cols] = carry[1].astype(o_ref.dtype)


def _sb_attention_sample(q, k, v, cache_k, cache_v, layer, batch):
    m = q.shape[0]
    t = m // batch
    assert t <= LANES
    q = q.reshape(batch, t, WIDTH)
    k, v = (a.reshape(DEPTH, batch, t, WIDTH) for a in (k, v))
    new_spec = pl.BlockSpec((None, t, WIDTH), lambda b: (b, 0, 0))
    new_kv_spec = pl.BlockSpec((None, None, t, WIDTH), lambda b: (layer, b, 0, 0))
    past_spec = pl.BlockSpec((None, None, PAST_LEN, WIDTH), lambda b: (layer, b, 0, 0))
    out = pl.pallas_call(
        _sb_sample_kernel,
        grid=(batch,),
        in_specs=[new_spec, new_kv_spec, new_kv_spec, past_spec, past_spec,
                  pl.BlockSpec((2 * LANES, 2 * LANES), lambda b: (0, 0))],
        out_specs=new_spec,
        out_shape=jax.ShapeDtypeStruct((batch, t, WIDTH), BF16),
        compiler_params=_params("arbitrary"),
        name="sb_attention_sample",
    )(q, k, v, cache_k, cache_v, _prefix_matrix())
    return out.reshape(m, WIDTH)


def _lambda_value(lamq_ref, lamk_ref, lam_init):
    e = jnp.exp(jnp.sum(lamq_ref[...] * lamk_ref[...], axis=-1, keepdims=True))
    first = lax.broadcasted_iota(jnp.int32, e.shape, 0) == 0
    return jnp.sum(jnp.where(first, e, -e), axis=0, keepdims=True) + lam_init


def _split_halves(q):
    lane = lax.broadcasted_iota(jnp.int32, q.shape, 1)
    zero = jnp.zeros_like(q)
    return jnp.where(lane < DIFF_QK_DIM, q, zero), jnp.where(lane >= DIFF_QK_DIM, q, zero)


def _diff_finish(o1, o2, lam, g, lam_init):
    return _rmsnorm_rows(o1 - lam * o2, g) * (1.0 - lam_init)


def _online_softmax_step(qz, k, v_ones, bias, state):
    m, acc = state
    s = _dot_nt(qz, k) + bias
    m_new = jnp.maximum(m, jnp.max(s, axis=-1, keepdims=True))
    p = jnp.exp2(s - m_new)
    return m_new, jnp.exp2(m - m_new) * acc + _dot(p.astype(BF16), v_ones)


def _diff_prompt_kernel(slopes_ref, q_ref, k_ref, v_ref, lamq_ref, lamk_ref, g_ref, o_ref, kb_ref, vb_ref, *,
                        lam_init):
    h = pl.program_id(1)
    qi = pl.program_id(2)

    @pl.when(qi == 0)
    def _():
        kb_ref[...] = k_ref[...].astype(BF16)
        vb_ref[:, :HEAD_DIM] = v_ref[...].astype(BF16)
        vb_ref[:, HEAD_DIM:] = jnp.ones((vb_ref.shape[0], HEAD_DIM), BF16)

    q1, q2 = _split_halves(q_ref[...])
    slope = slopes_ref[h]
    row = lax.broadcasted_iota(jnp.int32, (DIFF_Q_TILE, DIFF_KEY_STEP), 0)
    col = lax.broadcasted_iota(jnp.int32, (DIFF_Q_TILE, DIFF_KEY_STEP), 1)
    key_col = lax.broadcasted_iota(jnp.int32, (1, DIFF_KEY_STEP), 1)
    steps_per_tile = DIFF_Q_TILE // DIFF_KEY_STEP

    def step(kb, bias, state):
        off = pl.multiple_of(kb * DIFF_KEY_STEP, DIFF_KEY_STEP)
        k = kb_ref[pl.ds(off, DIFF_KEY_STEP), :]
        v_ones = vb_ref[pl.ds(off, DIFF_KEY_STEP), :]
        return tuple(_online_softmax_step(qz, k, v_ones, bias, st) for qz, st in zip((q1, q2), state))

    def full_step(kb, state):
        bias = slope * (kb * DIFF_KEY_STEP + key_col - qi * DIFF_Q_TILE).astype(F32)
        return step(kb, bias, state)

    one = (jnp.full((DIFF_Q_TILE, 1), NEG_BIG, F32), jnp.zeros((DIFF_Q_TILE, 2 * HEAD_DIM), F32))
    state = (one, one)
    for p in range(steps_per_tile):
        k_in_tile = col + p * DIFF_KEY_STEP
        visible = (k_in_tile >> CHUNK_SHIFT) <= (row >> CHUNK_SHIFT)
        bias = jnp.where(visible, slope * (row - jnp.abs(row - k_in_tile)).astype(F32), NEG_BIG)
        state = step(qi * steps_per_tile + p, bias, state)
    state = lax.fori_loop(0, qi * steps_per_tile, full_step, state)
    (_, a1), (_, a2) = state
    o = _diff_finish(a1[:, :HEAD_DIM] / a1[:, HEAD_DIM:], a2[:, :HEAD_DIM] / a2[:, HEAD_DIM:],
                     _lambda_value(lamq_ref, lamk_ref, lam_init), g_ref[...], lam_init)
    o_ref[...] = o.astype(o_ref.dtype)


def _diff_attention_prompt(q, k, v, layer, lam_params, g, slopes, lam_init, batch):
    assert DIFF_KEY_STEP % CHUNK == 0
    m = q.shape[0]
    t = m // batch
    q = q.reshape(batch, t, WIDTH)
    k, v = (a.reshape(DEPTH, batch, t, WIDTH) for a in (k, v))
    kv_spec = pl.BlockSpec((None, None, t, HEAD_DIM), lambda b, h, i: (layer, b, 0, h))
    q_spec = pl.BlockSpec((None, DIFF_Q_TILE, HEAD_DIM), lambda b, h, i: (b, i, h))
    lam_spec = pl.BlockSpec((2, DIFF_QK_DIM), lambda b, h, i: (0, 0))
    out = pl.pallas_call(
        functools.partial(_diff_prompt_kernel, lam_init=lam_init),
        grid=(batch, HEADS, t // DIFF_Q_TILE),
        in_specs=[pl.BlockSpec(memory_space=pltpu.SMEM), q_spec, kv_spec, kv_spec, lam_spec, lam_spec,
                  pl.BlockSpec((1, HEAD_DIM), lambda b, h, i: (0, 0))],
        out_specs=q_spec,
        out_shape=jax.ShapeDtypeStruct((batch, t, WIDTH), BF16),
        scratch_shapes=[pltpu.VMEM((t, HEAD_DIM), BF16), pltpu.VMEM((t, 2 * HEAD_DIM), BF16)],
        compiler_params=_params("arbitrary", "arbitrary", "arbitrary"),
        name="diff_attention_prompt",
    )(slopes, q, k, v, *lam_params, g)
    return out.reshape(m, WIDTH)


def _diff_sample_kernel(slopes_ref, q_ref, k_ref, v_ref, pk_ref, pv_ref, lamq_ref, lamk_ref, g_ref, o_ref, *,
                        lam_init):
    t = q_ref.shape[0]
    lam = _lambda_value(lamq_ref, lamk_ref, lam_init)
    g = g_ref[...]
    past_col = lax.broadcasted_iota(jnp.int32, (1, PAST_LEN), 1)
    row = lax.broadcasted_iota(jnp.int32, (t, t), 0)
    col = lax.broadcasted_iota(jnp.int32, (t, t), 1)
    past_dist = (past_col - PAST_LEN).astype(F32)
    new_dist = (row - jnp.abs(row - col)).astype(F32)
    for h in range(HEADS):
        cols = slice(h * HEAD_DIM, (h + 1) * HEAD_DIM)
        slope = slopes_ref[h]
        k_past = pk_ref[:, cols].astype(BF16)
        v_past = pv_ref[:, cols].astype(BF16)
        k_new = k_ref[:, cols].astype(BF16)
        v_new = v_ref[:, cols].astype(BF16)
        outs = []
        for qz in _split_halves(q_ref[:, cols]):
            s_past = _dot_nt(qz, k_past) + slope * past_dist
            s_new = _dot_nt(qz, k_new) + slope * new_dist
            m = jnp.maximum(jnp.max(s_past, axis=-1, keepdims=True), jnp.max(s_new, axis=-1, keepdims=True))
            p_past = jnp.exp2(s_past - m)
            p_new = jnp.exp2(s_new - m)
            total = jnp.sum(p_past, axis=-1, keepdims=True) + jnp.sum(p_new, axis=-1, keepdims=True)
            outs.append((_dot(p_past.astype(BF16), v_past) + _dot(p_new.astype(BF16), v_new)) / total)
        o_ref[:, cols] = _diff_finish(outs[0], outs[1], lam, g, lam_init).astype(o_ref.dtype)


def _diff_attention_sample(q, k, v, cache_k, cache_v, layer, lam_params, g, slopes, lam_init, batch):
    m = q.shape[0]
    t = m // batch
    assert t == CHUNK and PAST_LEN % CHUNK == 0
    q = q.reshape(batch, t, WIDTH)
    k, v = (a.reshape(DEPTH, batch, t, WIDTH) for a in (k, v))
    new_spec = pl.BlockSpec((None, t, WIDTH), lambda b: (b, 0, 0))
    new_kv_spec = pl.BlockSpec((None, None, t, WIDTH), lambda b: (layer, b, 0, 0))
    past_spec = pl.BlockSpec((None, None, PAST_LEN, WIDTH), lambda b: (layer, b, 0, 0))
    lam_spec = pl.BlockSpec((2, DIFF_QK_DIM), lambda b: (0, 0))
    out = pl.pallas_call(
        functools.partial(_diff_sample_kernel, lam_init=lam_init),
        grid=(batch,),
        in_specs=[pl.BlockSpec(memory_space=pltpu.SMEM), new_spec, new_kv_spec, new_kv_spec, past_spec, past_spec,
                  lam_spec, lam_spec, pl.BlockSpec((1, HEAD_DIM), lambda b: (0, 0))],
        out_specs=new_spec,
        out_shape=jax.ShapeDtypeStruct((batch, t, WIDTH), BF16),
        compiler_params=_params("arbitrary"),
        name="diff_attention_sample",
    )(slopes, q, k, v, cache_k, cache_v, *lam_params, g)
    return out.reshape(m, WIDTH)


def _sigmoid(x):
    return 1.0 / (1.0 + jnp.exp(-x))


def _merge_kernel(osb_ref, odf_ref, gates_ref, x_ref, wsb_ref, wdf_ref, wout_ref, o_ref, merged_ref):
    osb = osb_ref[...]
    odf = odf_ref[...]
    for c in range(D_MODEL // MERGE_COL_CHUNK):
        cols = slice(c * MERGE_COL_CHUNK, (c + 1) * MERGE_COL_CHUNK)
        gcols = slice(D_MODEL + c * MERGE_COL_CHUNK, D_MODEL + (c + 1) * MERGE_COL_CHUNK)
        g_sb = _sigmoid(gates_ref[:, cols].astype(F32))
        g_df = _sigmoid(gates_ref[:, gcols].astype(F32))
        merged = g_sb * _dot(osb, wsb_ref[:, cols]) + g_df * _dot(odf, wdf_ref[:, cols])
        merged_ref[:, cols] = merged.astype(BF16)
    o_ref[...] = x_ref[...] + _dot(merged_ref[...], wout_ref[...])


def _merge_project(o_sb, o_df, gates, x, w_sb, w_df, w_out):
    m = x.shape[0]
    tm = min(M_TILE, m)
    row_spec = lambda width: pl.BlockSpec((tm, width), lambda i: (i, 0))
    whole = lambda a: pl.BlockSpec(a.shape, lambda i: (0, 0), pipeline_mode=pl.Buffered(1))
    return pl.pallas_call(
        _merge_kernel,
        grid=(m // tm,),
        in_specs=[row_spec(WIDTH), row_spec(WIDTH), row_spec(2 * D_MODEL), row_spec(D_MODEL),
                  whole(w_sb), whole(w_df), whole(w_out)],
        out_specs=row_spec(D_MODEL),
        out_shape=jax.ShapeDtypeStruct((m, D_MODEL), F32),
        scratch_shapes=[pltpu.VMEM((tm, D_MODEL), BF16)],
        compiler_params=_params("arbitrary"),
        name="merge_project",
    )(o_sb, o_df, gates, x, w_sb, w_df, w_out)


def _mlp_kernel(x_ref, g_ref, wup_ref, wdown_ref, gf_ref, o_ref, h_ref, acc_ref, *, final_norm):
    f = pl.program_id(1)

    @pl.when(f == 0)
    def _():
        h_ref[...] = _rmsnorm_rows(x_ref[...], g_ref[...]).astype(BF16)
        acc_ref[...] = jnp.zeros_like(acc_ref)

    u = jnp.maximum(_dot(h_ref[...], wup_ref[...]), 0.0)
    acc_ref[...] += _dot((u * u).astype(BF16), wdown_ref[...])

    @pl.when(f == pl.num_programs(1) - 1)
    def _():
        y = x_ref[...] + acc_ref[...]
        if final_norm:
            y = _rmsnorm_rows(y, gf_ref[...])
        o_ref[...] = y


def _mlp(x, g, w_up, w_down, g_final, final_norm):
    m = x.shape[0]
    tm = min(M_TILE, m)
    vec_spec = pl.BlockSpec((1, D_MODEL), lambda i, f: (0, 0))
    return pl.pallas_call(
        functools.partial(_mlp_kernel, final_norm=final_norm),
        grid=(m // tm, D_FF // F_TILE),
        in_specs=[pl.BlockSpec((tm, D_MODEL), lambda i, f: (i, 0)), vec_spec,
                  pl.BlockSpec((D_MODEL, F_TILE), lambda i, f: (0, f)),
                  pl.BlockSpec((F_TILE, D_MODEL), lambda i, f: (f, 0)), vec_spec],
        out_specs=pl.BlockSpec((tm, D_MODEL), lambda i, f: (i, 0)),
        out_shape=jax.ShapeDtypeStruct((m, D_MODEL), F32),
        scratch_shapes=[pltpu.VMEM((tm, D_MODEL), BF16), pltpu.VMEM((tm, D_MODEL), F32)],
        compiler_params=_params("arbitrary", "arbitrary"),
        name="mlp",
    )(x, g, w_up, w_down, g_final)


def _trunk(x, caches, weights, slopes):
    batch, t, _ = x.shape
    x = x.reshape(batch * t, D_MODEL)
    kv_rows = tuple(jnp.zeros((DEPTH, batch * t, WIDTH), F32) for _ in range(4))
    for layer in range(DEPTH):
        w = {name: value[layer] for name, value in weights.items() if name != "final_norm_g"}
        lam_init = 0.8 - 0.6 * math.exp(-0.3 * layer)
        q_sb, k_sb, v_sb, q_df, k_df, v_df, gates = _in_projection(x, w["norm1_g"][None], w["w_in"], layer,
                                                                   kv_rows)
        kv_rows = (k_sb, v_sb, k_df, v_df)
        lam_params = (jnp.stack([w["lambda_q1"], w["lambda_q2"]]), jnp.stack([w["lambda_k1"], w["lambda_k2"]]))
        g_diff = w["diff_norm_g"][None]
        if caches is None:
            o_sb = _sb_attention_prompt(q_sb, k_sb, v_sb, layer, batch)
            o_df = _diff_attention_prompt(q_df, k_df, v_df, layer, lam_params, g_diff, slopes, lam_init, batch)
        else:
            c_sb_k, c_sb_v, c_df_k, c_df_v = caches
            o_sb = _sb_attention_sample(q_sb, k_sb, v_sb, c_sb_k, c_sb_v, layer, batch)
            o_df = _diff_attention_sample(q_df, k_df, v_df, c_df_k, c_df_v, layer, lam_params, g_diff,
                                          slopes, lam_init, batch)
        x = _merge_project(o_sb, o_df, gates, x, w["w_branch_sb"], w["w_branch_diff"], w["w_out"])
        x = _mlp(x, w["norm2_g"][None], w["w_up"], w["w_down"], weights["final_norm_g"][None],
                 final_norm=(layer == DEPTH - 1))
    stacked = tuple(r.reshape(DEPTH, batch, t, HEADS, HEAD_DIM) for r in kv_rows)
    return x.reshape(batch, t, D_MODEL), stacked


def kernel(x_prompt, x_sample, cache_sb_k, cache_sb_v, cache_diff_k, cache_diff_v, norm1_g, w_in, lambda_q1, lambda_k1, lambda_q2, lambda_k2, diff_norm_g, w_branch_sb, w_branch_diff, w_out, norm2_g, w_up, w_down, final_norm_g):
    weights = dict(norm1_g=norm1_g, w_in=w_in.astype(BF16), lambda_q1=lambda_q1, lambda_k1=lambda_k1,
                   lambda_q2=lambda_q2, lambda_k2=lambda_k2, diff_norm_g=diff_norm_g,
                   w_branch_sb=w_branch_sb.astype(BF16), w_branch_diff=w_branch_diff.astype(BF16),
                   w_out=w_out.astype(BF16), norm2_g=norm2_g, w_up=w_up.astype(BF16),
                   w_down=w_down.astype(BF16), final_norm_g=final_norm_g)
    slopes = jnp.exp2(-8.0 * jnp.arange(1, HEADS + 1, dtype=F32) / HEADS) * LOG2E
    caches = tuple(c.reshape(DEPTH, c.shape[1], PAST_LEN, WIDTH)
                   for c in (cache_sb_k, cache_sb_v, cache_diff_k, cache_diff_v))
    y_prompt, rows_p = _trunk(x_prompt, None, weights, slopes)
    y_sample, rows_s = _trunk(x_sample, caches, weights, slopes)
    return (y_prompt, y_sample) + rows_p + rows_s
```

```python
import functools
import math

import jax
import jax.numpy as jnp
from jax import lax
from jax.experimental import pallas as pl
from jax.experimental.pallas import tpu as pltpu

D_MODEL = 2048
DEPTH = 2
PAST_LEN = 1024
CHUNK = 64
CHUNK_SHIFT = CHUNK.bit_length() - 1
assert CHUNK == 1 << CHUNK_SHIFT
HEADS = 8
HEAD_DIM = 128
DIFF_QK_DIM = 64
WIDTH = HEADS * HEAD_DIM
D_FF = 4 * D_MODEL
IN_COLS = 6 * WIDTH + 2 * D_MODEL
EPS = 1e-6

LOG2E = math.log2(math.e)
SB_Q_SCALE = HEAD_DIM ** -0.5 * LOG2E
DIFF_Q_SCALE = DIFF_QK_DIM ** -0.5 * LOG2E
NEG_BIG = -1e30
SOFTPLUS_CLAMP = 30.0

LANES = 128
SB_Q_TILE = 512
SB_KEY_STEP = 512
DIFF_Q_TILE = 512
DIFF_KEY_STEP = 512
M_TILE = 512
N_TILE = 1024
F_TILE = 1024
MERGE_COL_CHUNK = 512

VMEM_LIMIT_BYTES = 56 * 1024 * 1024

F32 = jnp.float32
BF16 = jnp.bfloat16


def _params(*semantics):
    return pltpu.CompilerParams(dimension_semantics=semantics, vmem_limit_bytes=VMEM_LIMIT_BYTES)


def _dot(a, b):
    return jnp.dot(a, b, preferred_element_type=F32)


def _dot_nt(a, b):
    return lax.dot_general(a, b, (((1,), (1,)), ((), ())), preferred_element_type=F32)


def _rmsnorm_rows(x, g):
    return x * lax.rsqrt(jnp.mean(x * x, axis=-1, keepdims=True) + EPS) * g


def _inproj_kernel(x_ref, g_ref, w_ref, ksb_in, vsb_in, kdf_in, vdf_in,
                   qsb_ref, ksb_ref, vsb_ref, qdf_ref, kdf_ref, vdf_ref, gates_ref, h_ref):
    del ksb_in, vsb_in, kdf_in, vdf_in
    j = pl.program_id(1)

    @pl.when(j == 0)
    def _():
        h_ref[...] = _rmsnorm_rows(x_ref[...], g_ref[...]).astype(BF16)

    y = _dot(h_ref[...], w_ref[...])
    groups = ((qsb_ref, SB_Q_SCALE), (ksb_ref, None), (vsb_ref, None),
              (qdf_ref, DIFF_Q_SCALE), (kdf_ref, None), (vdf_ref, None))
    for i, (ref, scale) in enumerate(groups):
        @pl.when(j == i)
        def _(ref=ref, scale=scale):
            ref[...] = (y if scale is None else y * scale).astype(ref.dtype)

    @pl.when(j >= len(groups))
    def _():
        gates_ref[...] = y.astype(BF16)


def _in_projection(x, g, w_in, layer, kv_rows):
    m = x.shape[0]
    tm = min(M_TILE, m)
    q_shape = jax.ShapeDtypeStruct((m, WIDTH), BF16)
    kv_shape = jax.ShapeDtypeStruct((DEPTH, m, WIDTH), F32)
    q_spec = pl.BlockSpec((tm, N_TILE), lambda i, j: (i, 0))
    kv_spec = pl.BlockSpec((None, tm, N_TILE), lambda i, j: (layer, i, 0))
    return pl.pallas_call(
        _inproj_kernel,
        grid=(m // tm, IN_COLS // N_TILE),
        in_specs=[
            pl.BlockSpec((tm, D_MODEL), lambda i, j: (i, 0)),
            pl.BlockSpec((1, D_MODEL), lambda i, j: (0, 0)),
            pl.BlockSpec((D_MODEL, N_TILE), lambda i, j: (0, j)),
        ] + [pl.BlockSpec(memory_space=pl.ANY)] * 4,
        out_specs=[q_spec, kv_spec, kv_spec, q_spec, kv_spec, kv_spec,
                   pl.BlockSpec((tm, N_TILE), lambda i, j: (i, jnp.maximum(j - 6, 0)))],
        out_shape=[q_shape, kv_shape, kv_shape, q_shape, kv_shape, kv_shape,
                   jax.ShapeDtypeStruct((m, 2 * D_MODEL), BF16)],
        input_output_aliases={3: 1, 4: 2, 5: 4, 6: 5},
        scratch_shapes=[pltpu.VMEM((tm, D_MODEL), BF16)],
        compiler_params=_params("arbitrary", "arbitrary"),
        name="in_projection",
    )(x, g, w_in, *kv_rows)


def _prefix_matrix():
    s = jnp.arange(LANES)[:, None]
    j = jnp.arange(LANES)[None, :]
    half = jnp.concatenate([jnp.ones((LANES, LANES), BF16), (s > j).astype(BF16)], axis=1)
    return jnp.concatenate([half, half], axis=0)


def _softplus2(x):
    return jnp.maximum(x, jnp.log2(1.0 + jnp.exp2(jnp.minimum(x, SOFTPLUS_CLAMP))))


def _sb_step(q, k, v, pm, carry, visible):
    c, acc = carry
    z = _dot_nt(q, k)
    sp = _softplus2(z)
    log_beta = z - sp
    if visible is not None:
        sp = jnp.where(visible, sp, 0.0)
    hi = sp.astype(BF16)
    lo = (sp - hi.astype(F32)).astype(BF16)
    args = []
    for s in reversed(range(z.shape[1] // LANES)):
        cols = slice(s * LANES, (s + 1) * LANES)
        sums = _dot(jnp.concatenate([hi[:, cols], lo[:, cols]], axis=1), pm)
        args.insert(0, log_beta[:, cols] - sums[:, LANES:] - c)
        c = c + sums[:, :LANES]
    w = jnp.exp2(jnp.concatenate(args, axis=1))
    if visible is not None:
        w = jnp.where(visible, w, 0.0)
    return c, acc + _dot(w.astype(BF16), v)


def _sb_prompt_kernel(q_ref, k_ref, v_ref, pm_ref, o_ref, kb_ref, vb_ref):
    qi = pl.program_id(2)

    @pl.when(qi == 0)
    def _():
        kb_ref[...] = k_ref[...].astype(BF16)
        vb_ref[...] = v_ref[...].astype(BF16)

    q = q_ref[...]
    pm = pm_ref[...]
    row = lax.broadcasted_iota(jnp.int32, (SB_Q_TILE, SB_KEY_STEP), 0)
    col = lax.broadcasted_iota(jnp.int32, (SB_Q_TILE, SB_KEY_STEP), 1)
    steps_per_tile = SB_Q_TILE // SB_KEY_STEP

    def step(kb, carry, visible):
        off = pl.multiple_of(kb * SB_KEY_STEP, SB_KEY_STEP)
        return _sb_step(q, kb_ref[pl.ds(off, SB_KEY_STEP), :], vb_ref[pl.ds(off, SB_KEY_STEP), :],
                        pm, carry, visible)

    carry = (jnp.zeros((SB_Q_TILE, LANES), F32), jnp.zeros((SB_Q_TILE, HEAD_DIM), F32))
    for p in reversed(range(steps_per_tile)):
        carry = step(qi * steps_per_tile + p, carry, col + p * SB_KEY_STEP < row)
    n_full = qi * steps_per_tile
    carry = lax.fori_loop(0, n_full, lambda i, cr: step(n_full - 1 - i, cr, None), carry)
    o_ref[...] = carry[1].astype(o_ref.dtype)


def _sb_attention_prompt(q, k, v, layer, batch):
    m = q.shape[0]
    t = m // batch
    q = q.reshape(batch, t, WIDTH)
    k, v = (a.reshape(DEPTH, batch, t, WIDTH) for a in (k, v))
    kv_spec = pl.BlockSpec((None, None, t, HEAD_DIM), lambda b, h, i: (layer, b, 0, h))
    q_spec = pl.BlockSpec((None, SB_Q_TILE, HEAD_DIM), lambda b, h, i: (b, i, h))
    out = pl.pallas_call(
        _sb_prompt_kernel,
        grid=(batch, HEADS, t // SB_Q_TILE),
        in_specs=[q_spec, kv_spec, kv_spec,
                  pl.BlockSpec((2 * LANES, 2 * LANES), lambda b, h, i: (0, 0))],
        out_specs=q_spec,
        out_shape=jax.ShapeDtypeStruct((batch, t, WIDTH), BF16),
        scratch_shapes=[pltpu.VMEM((t, HEAD_DIM), BF16), pltpu.VMEM((t, HEAD_DIM), BF16)],
        compiler_params=_params("arbitrary", "arbitrary", "arbitrary"),
        name="sb_attention_prompt",
    )(q, k, v, _prefix_matrix())
    return out.reshape(m, WIDTH)


def _sb_sample_kernel(q_ref, k_ref, v_ref, pk_ref, pv_ref, pm_ref, o_ref):
    pm = pm_ref[...]
    t = q_ref.shape[0]
    row = lax.broadcasted_iota(jnp.int32, (t, LANES), 0)
    col = lax.broadcasted_iota(jnp.int32, (t, LANES), 1)
    pad = jnp.zeros((LANES - t, HEAD_DIM), BF16)
    for h in range(HEADS):
        cols = slice(h * HEAD_DIM, (h + 1) * HEAD_DIM)
        q = q_ref[:, cols]
        carry = (jnp.zeros((t, LANES), F32), jnp.zeros((t, HEAD_DIM), F32))
        k_new = jnp.concatenate([k_ref[:, cols].astype(BF16), pad], axis=0)
        v_new = jnp.concatenate([v_ref[:, cols].astype(BF16), pad], axis=0)
        carry = _sb_step(q, k_new, v_new, pm, carry, col < row)
        for s in reversed(range(PAST_LEN // SB_KEY_STEP)):
            keys = slice(s * SB_KEY_STEP, (s + 1) * SB_KEY_STEP)
            carry = _sb_step(q, pk_ref[keys, h, :].astype(BF16), pv_ref[keys, h, :].astype(BF16),
                             pm, carry, None)
        o_ref[:, cols] = carry[1].astype(o_ref.dtype)


def _sb_attention_sample(q, k, v, cache_k, cache_v, layer, batch):
    m = q.shape[0]
    t = m // batch
    assert t <= LANES
    q = q.reshape(batch, t, WIDTH)
    k, v = (a.reshape(DEPTH, batch, t, WIDTH) for a in (k, v))
    new_spec = pl.BlockSpec((None, t, WIDTH), lambda b: (b, 0, 0))
    new_kv_spec = pl.BlockSpec((None, None, t, WIDTH), lambda b: (layer, b, 0, 0))
    past_spec = pl.BlockSpec((None, None, PAST_LEN, HEADS, HEAD_DIM), lambda b: (layer, b, 0, 0, 0))
    out = pl.pallas_call(
        _sb_sample_kernel,
        grid=(batch,),
        in_specs=[new_spec, new_kv_spec, new_kv_spec, past_spec, past_spec,
                  pl.BlockSpec((2 * LANES, 2 * LANES), lambda b: (0, 0))],
        out_specs=new_spec,
        out_shape=jax.ShapeDtypeStruct((batch, t, WIDTH), BF16),
        compiler_params=_params("arbitrary"),
        name="sb_attention_sample",
    )(q, k, v, cache_k, cache_v, _prefix_matrix())
    return out.reshape(m, WIDTH)


def _lambda_value(lamq_ref, lamk_ref, lam_init):
    e = jnp.exp(jnp.sum(lamq_ref[...] * lamk_ref[...], axis=-1, keepdims=True))
    first = lax.broadcasted_iota(jnp.int32, e.shape, 0) == 0
    return jnp.sum(jnp.where(first, e, -e), axis=0, keepdims=True) + lam_init


def _split_halves(q):
    lane = lax.broadcasted_iota(jnp.int32, q.shape, 1)
    zero = jnp.zeros_like(q)
    return jnp.where(lane < DIFF_QK_DIM, q, zero), jnp.where(lane >= DIFF_QK_DIM, q, zero)


def _diff_finish(o1, o2, lam, g, lam_init):
    return _rmsnorm_rows(o1 - lam * o2, g) * (1.0 - lam_init)


def _online_softmax_step(qz, k, v_ones, bias, state):
    m, acc = state
    s = _dot_nt(qz, k) + bias
    m_new = jnp.maximum(m, jnp.max(s, axis=-1, keepdims=True))
    p = jnp.exp2(s - m_new)
    return m_new, jnp.exp2(m - m_new) * acc + _dot(p.astype(BF16), v_ones)


def _diff_prompt_kernel(slopes_ref, q_ref, k_ref, v_ref, lamq_ref, lamk_ref, g_ref, o_ref, kb_ref, vb_ref, *,
                        lam_init):
    h = pl.program_id(1)
    qi = pl.program_id(2)

    @pl.when(qi == 0)
    def _():
        kb_ref[...] = k_ref[...].astype(BF16)
        vb_ref[:, :HEAD_DIM] = v_ref[...].astype(BF16)
        vb_ref[:, HEAD_DIM:] = jnp.ones((vb_ref.shape[0], HEAD_DIM), BF16)

    q1, q2 = _split_halves(q_ref[...])
    slope = slopes_ref[h]
    row = lax.broadcasted_iota(jnp.int32, (DIFF_Q_TILE, DIFF_KEY_STEP), 0)
    col = lax.broadcasted_iota(jnp.int32, (DIFF_Q_TILE, DIFF_KEY_STEP), 1)
    key_col = lax.broadcasted_iota(jnp.int32, (1, DIFF_KEY_STEP), 1)
    steps_per_tile = DIFF_Q_TILE // DIFF_KEY_STEP

    def step(kb, bias, state):
        off = pl.multiple_of(kb * DIFF_KEY_STEP, DIFF_KEY_STEP)
        k = kb_ref[pl.ds(off, DIFF_KEY_STEP), :]
        v_ones = vb_ref[pl.ds(off, DIFF_KEY_STEP), :]
        return tuple(_online_softmax_step(qz, k, v_ones, bias, st) for qz, st in zip((q1, q2), state))

    def full_step(kb, state):
        bias = slope * (kb * DIFF_KEY_STEP + key_col - qi * DIFF_Q_TILE).astype(F32)
        return step(kb, bias, state)

    one = (jnp.full((DIFF_Q_TILE, 1), NEG_BIG, F32), jnp.zeros((DIFF_Q_TILE, 2 * HEAD_DIM), F32))
    state = (one, one)
    for p in range(steps_per_tile):
        k_in_tile = col + p * DIFF_KEY_STEP
        visible = (k_in_tile >> CHUNK_SHIFT) <= (row >> CHUNK_SHIFT)
        bias = jnp.where(visible, slope * (row - jnp.abs(row - k_in_tile)).astype(F32), NEG_BIG)
        state = step(qi * steps_per_tile + p, bias, state)
    state = lax.fori_loop(0, qi * steps_per_tile, full_step, state)
    (_, a1), (_, a2) = state
    o = _diff_finish(a1[:, :HEAD_DIM] / a1[:, HEAD_DIM:], a2[:, :HEAD_DIM] / a2[:, HEAD_DIM:],
                     _lambda_value(lamq_ref, lamk_ref, lam_init), g_ref[...], lam_init)
    o_ref[...] = o.astype(o_ref.dtype)


def _diff_attention_prompt(q, k, v, layer, lam_params, g, slopes, lam_init, batch):
    assert DIFF_KEY_STEP % CHUNK == 0
    m = q.shape[0]
    t = m // batch
    q = q.reshape(batch, t, WIDTH)
    k, v = (a.reshape(DEPTH, batch, t, WIDTH) for a in (k, v))
    kv_spec = pl.BlockSpec((None, None, t, HEAD_DIM), lambda b, h, i: (layer, b, 0, h))
    q_spec = pl.BlockSpec((None, DIFF_Q_TILE, HEAD_DIM), lambda b, h, i: (b, i, h))
    lam_spec = pl.BlockSpec((2, DIFF_QK_DIM), lambda b, h, i: (0, 0))
    out = pl.pallas_call(
        functools.partial(_diff_prompt_kernel, lam_init=lam_init),
        grid=(batch, HEADS, t // DIFF_Q_TILE),
        in_specs=[pl.BlockSpec(memory_space=pltpu.SMEM), q_spec, kv_spec, kv_spec, lam_spec, lam_spec,
                  pl.BlockSpec((1, HEAD_DIM), lambda b, h, i: (0, 0))],
        out_specs=q_spec,
        out_shape=jax.ShapeDtypeStruct((batch, t, WIDTH), BF16),
        scratch_shapes=[pltpu.VMEM((t, HEAD_DIM), BF16), pltpu.VMEM((t, 2 * HEAD_DIM), BF16)],
        compiler_params=_params("arbitrary", "arbitrary", "arbitrary"),
        name="diff_attention_prompt",
    )(slopes, q, k, v, *lam_params, g)
    return out.reshape(m, WIDTH)


def _diff_sample_kernel(slopes_ref, q_ref, k_ref, v_ref, pk_ref, pv_ref, lamq_ref, lamk_ref, g_ref, o_ref, *,
                        lam_init):
    t = q_ref.shape[0]
    lam = _lambda_value(lamq_ref, lamk_ref, lam_init)
    g = g_ref[...]
    past_col = lax.broadcasted_iota(jnp.int32, (1, PAST_LEN), 1)
    row = lax.broadcasted_iota(jnp.int32, (t, t), 0)
    col = lax.broadcasted_iota(jnp.int32, (t, t), 1)
    past_dist = (past_col - PAST_LEN).astype(F32)
    new_dist = (row - jnp.abs(row - col)).astype(F32)
    for h in range(HEADS):
        cols = slice(h * HEAD_DIM, (h + 1) * HEAD_DIM)
        slope = slopes_ref[h]
        k_past = pk_ref[:, h, :].astype(BF16)
        v_past = pv_ref[:, h, :].astype(BF16)
        k_new = k_ref[:, cols].astype(BF16)
        v_new = v_ref[:, cols].astype(BF16)
        outs = []
        for qz in _split_halves(q_ref[:, cols]):
            s_past = _dot_nt(qz, k_past) + slope * past_dist
            s_new = _dot_nt(qz, k_new) + slope * new_dist
            m = jnp.maximum(jnp.max(s_past, axis=-1, keepdims=True), jnp.max(s_new, axis=-1, keepdims=True))
            p_past = jnp.exp2(s_past - m)
            p_new = jnp.exp2(s_new - m)
            total = jnp.sum(p_past, axis=-1, keepdims=True) + jnp.sum(p_new, axis=-1, keepdims=True)
            outs.append((_dot(p_past.astype(BF16), v_past) + _dot(p_new.astype(BF16), v_new)) / total)
        o_ref[:, cols] = _diff_finish(outs[0], outs[1], lam, g, lam_init).astype(o_ref.dtype)


def _diff_attention_sample(q, k, v, cache_k, cache_v, layer, lam_params, g, slopes, lam_init, batch):
    m = q.shape[0]
    t = m // batch
    assert t == CHUNK and PAST_LEN % CHUNK == 0
    q = q.reshape(batch, t, WIDTH)
    k, v = (a.reshape(DEPTH, batch, t, WIDTH) for a in (k, v))
    new_spec = pl.BlockSpec((None, t, WIDTH), lambda b: (b, 0, 0))
    new_kv_spec = pl.BlockSpec((None, None, t, WIDTH), lambda b: (layer, b, 0, 0))
    past_spec = pl.BlockSpec((None, None, PAST_LEN, HEADS, HEAD_DIM), lambda b: (layer, b, 0, 0, 0))
    lam_spec = pl.BlockSpec((2, DIFF_QK_DIM), lambda b: (0, 0))
    out = pl.pallas_call(
        functools.partial(_diff_sample_kernel, lam_init=lam_init),
        grid=(batch,),
        in_specs=[pl.BlockSpec(memory_space=pltpu.SMEM), new_spec, new_kv_spec, new_kv_spec, past_spec, past_spec,
                  lam_spec, lam_spec, pl.BlockSpec((1, HEAD_DIM), lambda b: (0, 0))],
        out_specs=new_spec,
        out_shape=jax.ShapeDtypeStruct((batch, t, WIDTH), BF16),
        compiler_params=_params("arbitrary"),
        name="diff_attention_sample",
    )(slopes, q, k, v, cache_k, cache_v, *lam_params, g)
    return out.reshape(m, WIDTH)


def _sigmoid(x):
    return 1.0 / (1.0 + jnp.exp(-x))


def _merge_kernel(osb_ref, odf_ref, gates_ref, x_ref, wsb_ref, wdf_ref, wout_ref, o_ref, merged_ref):
    osb = osb_ref[...]
    odf = odf_ref[...]
    for c in range(D_MODEL // MERGE_COL_CHUNK):
        cols = slice(c * MERGE_COL_CHUNK, (c + 1) * MERGE_COL_CHUNK)
        gcols = slice(D_MODEL + c * MERGE_COL_CHUNK, D_MODEL + (c + 1) * MERGE_COL_CHUNK)
        g_sb = _sigmoid(gates_ref[:, cols].astype(F32))
        g_df = _sigmoid(gates_ref[:, gcols].astype(F32))
        merged = g_sb * _dot(osb, wsb_ref[:, cols]) + g_df * _dot(odf, wdf_ref[:, cols])
        merged_ref[:, cols] = merged.astype(BF16)
    o_ref[...] = x_ref[...] + _dot(merged_ref[...], wout_ref[...])


def _merge_project(o_sb, o_df, gates, x, w_sb, w_df, w_out):
    m = x.shape[0]
    tm = min(M_TILE, m)
    row_spec = lambda width: pl.BlockSpec((tm, width), lambda i: (i, 0))
    whole = lambda a: pl.BlockSpec(a.shape, lambda i: (0, 0), pipeline_mode=pl.Buffered(1))
    return pl.pallas_call(
        _merge_kernel,
        grid=(m // tm,),
        in_specs=[row_spec(WIDTH), row_spec(WIDTH), row_spec(2 * D_MODEL), row_spec(D_MODEL),
                  whole(w_sb), whole(w_df), whole(w_out)],
        out_specs=row_spec(D_MODEL),
        out_shape=jax.ShapeDtypeStruct((m, D_MODEL), F32),
        scratch_shapes=[pltpu.VMEM((tm, D_MODEL), BF16)],
        compiler_params=_params("arbitrary"),
        name="merge_project",
    )(o_sb, o_df, gates, x, w_sb, w_df, w_out)


def _mlp_kernel(x_ref, g_ref, wup_ref, wdown_ref, gf_ref, o_ref, h_ref, acc_ref, *, final_norm):
    f = pl.program_id(1)

    @pl.when(f == 0)
    def _():
        h_ref[...] = _rmsnorm_rows(x_ref[...], g_ref[...]).astype(BF16)
        acc_ref[...] = jnp.zeros_like(acc_ref)

    u = jnp.maximum(_dot(h_ref[...], wup_ref[...]), 0.0)
    acc_ref[...] += _dot((u * u).astype(BF16), wdown_ref[...])

    @pl.when(f == pl.num_programs(1) - 1)
    def _():
        y = x_ref[...] + acc_ref[...]
        if final_norm:
            y = _rmsnorm_rows(y, gf_ref[...])
        o_ref[...] = y


def _mlp(x, g, w_up, w_down, g_final, final_norm):
    m = x.shape[0]
    tm = min(M_TILE, m)
    vec_spec = pl.BlockSpec((1, D_MODEL), lambda i, f: (0, 0))
    return pl.pallas_call(
        functools.partial(_mlp_kernel, final_norm=final_norm),
        grid=(m // tm, D_FF // F_TILE),
        in_specs=[pl.BlockSpec((tm, D_MODEL), lambda i, f: (i, 0)), vec_spec,
                  pl.BlockSpec((D_MODEL, F_TILE), lambda i, f: (0, f)),
                  pl.BlockSpec((F_TILE, D_MODEL), lambda i, f: (f, 0)), vec_spec],
        out_specs=pl.BlockSpec((tm, D_MODEL), lambda i, f: (i, 0)),
        out_shape=jax.ShapeDtypeStruct((m, D_MODEL), F32),
        scratch_shapes=[pltpu.VMEM((tm, D_MODEL), BF16), pltpu.VMEM((tm, D_MODEL), F32)],
        compiler_params=_params("arbitrary", "arbitrary"),
        name="mlp",
    )(x, g, w_up, w_down, g_final)


def _trunk(x, caches, weights, slopes):
    batch, t, _ = x.shape
    x = x.reshape(batch * t, D_MODEL)
    kv_rows = tuple(jnp.zeros((DEPTH, batch * t, WIDTH), F32) for _ in range(4))
    for layer in range(DEPTH):
        w = {name: value[layer] for name, value in weights.items() if name != "final_norm_g"}
        lam_init = 0.8 - 0.6 * math.exp(-0.3 * layer)
        q_sb, k_sb, v_sb, q_df, k_df, v_df, gates = _in_projection(x, w["norm1_g"][None], w["w_in"], layer,
                                                                   kv_rows)
        kv_rows = (k_sb, v_sb, k_df, v_df)
        lam_params = (jnp.stack([w["lambda_q1"], w["lambda_q2"]]), jnp.stack([w["lambda_k1"], w["lambda_k2"]]))
        g_diff = w["diff_norm_g"][None]
        if caches is None:
            o_sb = _sb_attention_prompt(q_sb, k_sb, v_sb, layer, batch)
            o_df = _diff_attention_prompt(q_df, k_df, v_df, layer, lam_params, g_diff, slopes, lam_init, batch)
        else:
            c_sb_k, c_sb_v, c_df_k, c_df_v = caches
            o_sb = _sb_attention_sample(q_sb, k_sb, v_sb, c_sb_k, c_sb_v, layer, batch)
            o_df = _diff_attention_sample(q_df, k_df, v_df, c_df_k, c_df_v, layer, lam_params, g_diff,
                                          slopes, lam_init, batch)
        x = _merge_project(o_sb, o_df, gates, x, w["w_branch_sb"], w["w_branch_diff"], w["w_out"])
        x = _mlp(x, w["norm2_g"][None], w["w_up"], w["w_down"], weights["final_norm_g"][None],
                 final_norm=(layer == DEPTH - 1))
    stacked = tuple(r.reshape(DEPTH, batch, t, HEADS, HEAD_DIM) for r in kv_rows)
    return x.reshape(batch, t, D_MODEL), stacked


def kernel(x_prompt, x_sample, cache_sb_k, cache_sb_v, cache_diff_k, cache_diff_v, norm1_g, w_in, lambda_q1, lambda_k1, lambda_q2, lambda_k2, diff_norm_g, w_branch_sb, w_branch_diff, w_out, norm2_g, w_up, w_down, final_norm_g):
    weights = dict(norm1_g=norm1_g, w_in=w_in.astype(BF16), lambda_q1=lambda_q1, lambda_k1=lambda_k1,
                   lambda_q2=lambda_q2, lambda_k2=lambda_k2, diff_norm_g=diff_norm_g,
                   w_branch_sb=w_branch_sb.astype(BF16), w_branch_diff=w_branch_diff.astype(BF16),
                   w_out=w_out.astype(BF16), norm2_g=norm2_g, w_up=w_up.astype(BF16),
                   w_down=w_down.astype(BF16), final_norm_g=final_norm_g)
    slopes = jnp.exp2(-8.0 * jnp.arange(1, HEADS + 1, dtype=F32) / HEADS) * LOG2E
    caches = (cache_sb_k, cache_sb_v, cache_diff_k, cache_diff_v)
    y_prompt, rows_p = _trunk(x_prompt, None, weights, slopes)
    y_sample, rows_s = _trunk(x_sample, caches, weights, slopes)
    return (y_prompt, y_sample) + rows_p + rows_s
```

```python
import functools
import math

import jax
import jax.numpy as jnp
from jax import lax
from jax.experimental import pallas as pl
from jax.experimental.pallas import tpu as pltpu

D_MODEL = 2048
DEPTH = 2
PAST_LEN = 1024
CHUNK = 64
CHUNK_SHIFT = CHUNK.bit_length() - 1
assert CHUNK == 1 << CHUNK_SHIFT
HEADS = 8
HEAD_DIM = 128
DIFF_QK_DIM = 64
WIDTH = HEADS * HEAD_DIM
D_FF = 4 * D_MODEL
IN_COLS = 6 * WIDTH + 2 * D_MODEL
EPS = 1e-6

LOG2E = math.log2(math.e)
SB_Q_SCALE = HEAD_DIM ** -0.5 * LOG2E
DIFF_Q_SCALE = DIFF_QK_DIM ** -0.5 * LOG2E
NEG_BIG = -1e30
SOFTPLUS_CLAMP = 30.0

LANES = 128
SB_Q_TILE = 512
SB_KEY_STEP = 512
DIFF_Q_TILE = 512
DIFF_KEY_STEP = 512
M_TILE = 512
N_TILE = 1024
F_TILE = 1024
MERGE_COL_CHUNK = 512

VMEM_LIMIT_BYTES = 56 * 1024 * 1024

F32 = jnp.float32
BF16 = jnp.bfloat16


def _params(*semantics):
    return pltpu.CompilerParams(dimension_semantics=semantics, vmem_limit_bytes=VMEM_LIMIT_BYTES)


def _dot(a, b):
    return jnp.dot(a, b, preferred_element_type=F32)


def _dot_nt(a, b):
    return lax.dot_general(a, b, (((1,), (1,)), ((), ())), preferred_element_type=F32)


def _rmsnorm_rows(x, g):
    return x * lax.rsqrt(jnp.mean(x * x, axis=-1, keepdims=True) + EPS) * g


def _inproj_kernel(x_ref, g_ref, w_ref, ksb_in, vsb_in, kdf_in, vdf_in,
                   qsb_ref, ksb_ref, vsb_ref, qdf_ref, kdf_ref, vdf_ref, gates_ref, h_ref):
    del ksb_in, vsb_in, kdf_in, vdf_in
    j = pl.program_id(1)

    @pl.when(j == 0)
    def _():
        h_ref[...] = _rmsnorm_rows(x_ref[...], g_ref[...]).astype(BF16)

    groups = ((qsb_ref, SB_Q_SCALE), (ksb_ref, None), (vsb_ref, None),
              (qdf_ref, DIFF_Q_SCALE), (kdf_ref, None), (vdf_ref, None))
    for i, (ref, scale) in enumerate(groups):
        @pl.when(j == i)
        def _(ref=ref, scale=scale):
            y = _dot(h_ref[...], w_ref[...])
            ref[...] = (y if scale is None else y * scale).astype(ref.dtype)

    @pl.when(j >= len(groups))
    def _():
        gates_ref[...] = _dot(h_ref[...], w_ref[...]).astype(BF16)


def _in_projection(x, g, w_in, layer, kv_rows):
    m = x.shape[0]
    tm = min(M_TILE, m)
    q_shape = jax.ShapeDtypeStruct((m, WIDTH), BF16)
    kv_shape = jax.ShapeDtypeStruct((DEPTH, m, WIDTH), F32)
    q_spec = pl.BlockSpec((tm, N_TILE), lambda i, j: (i, 0))
    kv_spec = pl.BlockSpec((None, tm, N_TILE), lambda i, j: (layer, i, 0))
    return pl.pallas_call(
        _inproj_kernel,
        grid=(m // tm, IN_COLS // N_TILE),
        in_specs=[
            pl.BlockSpec((tm, D_MODEL), lambda i, j: (i, 0)),
            pl.BlockSpec((1, D_MODEL), lambda i, j: (0, 0)),
            pl.BlockSpec((D_MODEL, N_TILE), lambda i, j: (0, j)),
        ] + [pl.BlockSpec(memory_space=pl.ANY)] * 4,
        out_specs=[q_spec, kv_spec, kv_spec, q_spec, kv_spec, kv_spec,
                   pl.BlockSpec((tm, N_TILE), lambda i, j: (i, jnp.maximum(j - 6, 0)))],
        out_shape=[q_shape, kv_shape, kv_shape, q_shape, kv_shape, kv_shape,
                   jax.ShapeDtypeStruct((m, 2 * D_MODEL), BF16)],
        input_output_aliases={3: 1, 4: 2, 5: 4, 6: 5},
        scratch_shapes=[pltpu.VMEM((tm, D_MODEL), BF16)],
        compiler_params=_params("arbitrary", "arbitrary"),
        name="in_projection",
    )(x, g, w_in, *kv_rows)


def _prefix_matrix():
    s = jnp.arange(LANES)[:, None]
    j = jnp.arange(LANES)[None, :]
    half = jnp.concatenate([jnp.ones((LANES, LANES), BF16), (s > j).astype(BF16)], axis=1)
    return jnp.concatenate([half, half], axis=0)


def _softplus2(x):
    return jnp.maximum(x, jnp.log2(1.0 + jnp.exp2(jnp.minimum(x, SOFTPLUS_CLAMP))))


def _sb_step(q, k, v, pm, carry, visible):
    c, acc = carry
    z = _dot_nt(q, k)
    sp = _softplus2(z)
    log_beta = z - sp
    if visible is not None:
        sp = jnp.where(visible, sp, 0.0)
    hi = sp.astype(BF16)
    lo = (sp - hi.astype(F32)).astype(BF16)
    args = []
    for s in reversed(range(z.shape[1] // LANES)):
        cols = slice(s * LANES, (s + 1) * LANES)
        sums = _dot(jnp.concatenate([hi[:, cols], lo[:, cols]], axis=1), pm)
        args.insert(0, log_beta[:, cols] - sums[:, LANES:] - c)
        c = c + sums[:, :LANES]
    w = jnp.exp2(jnp.concatenate(args, axis=1))
    if visible is not None:
        w = jnp.where(visible, w, 0.0)
    return c, acc + _dot(w.astype(BF16), v)


def _sb_prompt_kernel(q_ref, k_ref, v_ref, pm_ref, o_ref, kb_ref, vb_ref):
    qi = pl.program_id(2)

    @pl.when(qi == 0)
    def _():
        kb_ref[...] = k_ref[...].astype(BF16)
        vb_ref[...] = v_ref[...].astype(BF16)

    q = q_ref[...]
    pm = pm_ref[...]
    row = lax.broadcasted_iota(jnp.int32, (SB_Q_TILE, SB_KEY_STEP), 0)
    col = lax.broadcasted_iota(jnp.int32, (SB_Q_TILE, SB_KEY_STEP), 1)
    steps_per_tile = SB_Q_TILE // SB_KEY_STEP

    def step(kb, carry, visible):
        off = pl.multiple_of(kb * SB_KEY_STEP, SB_KEY_STEP)
        return _sb_step(q, kb_ref[pl.ds(off, SB_KEY_STEP), :], vb_ref[pl.ds(off, SB_KEY_STEP), :],
                        pm, carry, visible)

    carry = (jnp.zeros((SB_Q_TILE, LANES), F32), jnp.zeros((SB_Q_TILE, HEAD_DIM), F32))
    for p in reversed(range(steps_per_tile)):
        carry = step(qi * steps_per_tile + p, carry, col + p * SB_KEY_STEP < row)
    n_full = qi * steps_per_tile
    carry = lax.fori_loop(0, n_full, lambda i, cr: step(n_full - 1 - i, cr, None), carry)
    o_ref[...] = carry[1].astype(o_ref.dtype)


def _sb_attention_prompt(q, k, v, layer, batch):
    m = q.shape[0]
    t = m // batch
    q = q.reshape(batch, t, WIDTH)
    k, v = (a.reshape(DEPTH, batch, t, WIDTH) for a in (k, v))
    kv_spec = pl.BlockSpec((None, None, t, HEAD_DIM), lambda b, h, i: (layer, b, 0, h))
    q_spec = pl.BlockSpec((None, SB_Q_TILE, HEAD_DIM), lambda b, h, i: (b, i, h))
    out = pl.pallas_call(
        _sb_prompt_kernel,
        grid=(batch, HEADS, t // SB_Q_TILE),
        in_specs=[q_spec, kv_spec, kv_spec,
                  pl.BlockSpec((2 * LANES, 2 * LANES), lambda b, h, i: (0, 0))],
        out_specs=q_spec,
        out_shape=jax.ShapeDtypeStruct((batch, t, WIDTH), BF16),
        scratch_shapes=[pltpu.VMEM((t, HEAD_DIM), BF16), pltpu.VMEM((t, HEAD_DIM), BF16)],
        compiler_params=_params("arbitrary", "arbitrary", "arbitrary"),
        name="sb_attention_prompt",
    )(q, k, v, _prefix_matrix())
    return out.reshape(m, WIDTH)


def _sb_sample_kernel(q_ref, k_ref, v_ref, pk_ref, pv_ref, pm_ref, o_ref):
    pm = pm_ref[...]
    t = q_ref.shape[0]
    row = lax.broadcasted_iota(jnp.int32, (t, LANES), 0)
    col = lax.broadcasted_iota(jnp.int32, (t, LANES), 1)
    pad = jnp.zeros((LANES - t, HEAD_DIM), BF16)
    for h in range(HEADS):
        cols = slice(h * HEAD_DIM, (h + 1) * HEAD_DIM)
        q = q_ref[:, cols]
        carry = (jnp.zeros((t, LANES), F32), jnp.zeros((t, HEAD_DIM), F32))
        k_new = jnp.concatenate([k_ref[:, cols].astype(BF16), pad], axis=0)
        v_new = jnp.concatenate([v_ref[:, cols].astype(BF16), pad], axis=0)
        carry = _sb_step(q, k_new, v_new, pm, carry, col < row)
        for s in reversed(range(PAST_LEN // SB_KEY_STEP)):
            keys = slice(s * SB_KEY_STEP, (s + 1) * SB_KEY_STEP)
            carry = _sb_step(q, pk_ref[keys, h, :].astype(BF16), pv_ref[keys, h, :].astype(BF16),
                             pm, carry, None)
        o_ref[:, cols] = carry[1].astype(o_ref.dtype)


def _sb_attention_sample(q, k, v, cache_k, cache_v, layer, batch):
    m = q.shape[0]
    t = m // batch
    assert t <= LANES
    q = q.reshape(batch, t, WIDTH)
    k, v = (a.reshape(DEPTH, batch, t, WIDTH) for a in (k, v))
    new_spec = pl.BlockSpec((None, t, WIDTH), lambda b: (b, 0, 0))
    new_kv_spec = pl.BlockSpec((None, None, t, WIDTH), lambda b: (layer, b, 0, 0))
    past_spec = pl.BlockSpec((None, None, PAST_LEN, HEADS, HEAD_DIM), lambda b: (layer, b, 0, 0, 0))
    out = pl.pallas_call(
        _sb_sample_kernel,
        grid=(batch,),
        in_specs=[new_spec, new_kv_spec, new_kv_spec, past_spec, past_spec,
                  pl.BlockSpec((2 * LANES, 2 * LANES), lambda b: (0, 0))],
        out_specs=new_spec,
        out_shape=jax.ShapeDtypeStruct((batch, t, WIDTH), BF16),
        compiler_params=_params("arbitrary"),
        name="sb_attention_sample",
    )(q, k, v, cache_k, cache_v, _prefix_matrix())
    return out.reshape(m, WIDTH)


def _lambda_value(lamq_ref, lamk_ref, lam_init):
    e = jnp.exp(jnp.sum(lamq_ref[...] * lamk_ref[...], axis=-1, keepdims=True))
    first = lax.broadcasted_iota(jnp.int32, e.shape, 0) == 0
    return jnp.sum(jnp.where(first, e, -e), axis=0, keepdims=True) + lam_init


def _split_halves(q):
    lane = lax.broadcasted_iota(jnp.int32, q.shape, 1)
    zero = jnp.zeros_like(q)
    return jnp.where(lane < DIFF_QK_DIM, q, zero), jnp.where(lane >= DIFF_QK_DIM, q, zero)


def _diff_finish(o1, o2, lam, g, lam_init):
    return _rmsnorm_rows(o1 - lam * o2, g) * (1.0 - lam_init)


def _online_softmax_step(qz, k, v_ones, bias, state):
    m, acc = state
    s = _dot_nt(qz, k) + bias
    m_new = jnp.maximum(m, jnp.max(s, axis=-1, keepdims=True))
    p = jnp.exp2(s - m_new)
    return m_new, jnp.exp2(m - m_new) * acc + _dot(p.astype(BF16), v_ones)


def _diff_prompt_kernel(slopes_ref, q_ref, k_ref, v_ref, lamq_ref, lamk_ref, g_ref, o_ref, kb_ref, vb_ref, *,
                        lam_init):
    h = pl.program_id(1)
    qi = pl.program_id(2)

    @pl.when(qi == 0)
    def _():
        kb_ref[...] = k_ref[...].astype(BF16)
        vb_ref[:, :HEAD_DIM] = v_ref[...].astype(BF16)
        vb_ref[:, HEAD_DIM:] = jnp.ones((vb_ref.shape[0], HEAD_DIM), BF16)

    q1, q2 = _split_halves(q_ref[...])
    slope = slopes_ref[h]
    row = lax.broadcasted_iota(jnp.int32, (DIFF_Q_TILE, DIFF_KEY_STEP), 0)
    col = lax.broadcasted_iota(jnp.int32, (DIFF_Q_TILE, DIFF_KEY_STEP), 1)
    key_col = lax.broadcasted_iota(jnp.int32, (1, DIFF_KEY_STEP), 1)
    steps_per_tile = DIFF_Q_TILE // DIFF_KEY_STEP

    def step(kb, bias, state):
        off = pl.multiple_of(kb * DIFF_KEY_STEP, DIFF_KEY_STEP)
        k = kb_ref[pl.ds(off, DIFF_KEY_STEP), :]
        v_ones = vb_ref[pl.ds(off, DIFF_KEY_STEP), :]
        return tuple(_online_softmax_step(qz, k, v_ones, bias, st) for qz, st in zip((q1, q2), state))

    def full_step(kb, state):
        bias = slope * (kb * DIFF_KEY_STEP + key_col - qi * DIFF_Q_TILE).astype(F32)
        return step(kb, bias, state)

    one = (jnp.full((DIFF_Q_TILE, 1), NEG_BIG, F32), jnp.zeros((DIFF_Q_TILE, 2 * HEAD_DIM), F32))
    state = (one, one)
    for p in range(steps_per_tile):
        k_in_tile = col + p * DIFF_KEY_STEP
        visible = (k_in_tile >> CHUNK_SHIFT) <= (row >> CHUNK_SHIFT)
        bias = jnp.where(visible, slope * (row - jnp.abs(row - k_in_tile)).astype(F32), NEG_BIG)
        state = step(qi * steps_per_tile + p, bias, state)
    state = lax.fori_loop(0, qi * steps_per_tile, full_step, state)
    (_, a1), (_, a2) = state
    o = _diff_finish(a1[:, :HEAD_DIM] / a1[:, HEAD_DIM:], a2[:, :HEAD_DIM] / a2[:, HEAD_DIM:],
                     _lambda_value(lamq_ref, lamk_ref, lam_init), g_ref[...], lam_init)
    o_ref[...] = o.astype(o_ref.dtype)


def _diff_attention_prompt(q, k, v, layer, lam_params, g, slopes, lam_init, batch):
    assert DIFF_KEY_STEP % CHUNK == 0
    m = q.shape[0]
    t = m // batch
    q = q.reshape(batch, t, WIDTH)
    k, v = (a.reshape(DEPTH, batch, t, WIDTH) for a in (k, v))
    kv_spec = pl.BlockSpec((None, None, t, HEAD_DIM), lambda b, h, i: (layer, b, 0, h))
    q_spec = pl.BlockSpec((None, DIFF_Q_TILE, HEAD_DIM), lambda b, h, i: (b, i, h))
    lam_spec = pl.BlockSpec((2, DIFF_QK_DIM), lambda b, h, i: (0, 0))
    out = pl.pallas_call(
        functools.partial(_diff_prompt_kernel, lam_init=lam_init),
        grid=(batch, HEADS, t // DIFF_Q_TILE),
        in_specs=[pl.BlockSpec(memory_space=pltpu.SMEM), q_spec, kv_spec, kv_spec, lam_spec, lam_spec,
                  pl.BlockSpec((1, HEAD_DIM), lambda b, h, i: (0, 0))],
        out_specs=q_spec,
        out_shape=jax.ShapeDtypeStruct((batch, t, WIDTH), BF16),
        scratch_shapes=[pltpu.VMEM((t, HEAD_DIM), BF16), pltpu.VMEM((t, 2 * HEAD_DIM), BF16)],
        compiler_params=_params("arbitrary", "arbitrary", "arbitrary"),
        name="diff_attention_prompt",
    )(slopes, q, k, v, *lam_params, g)
    return out.reshape(m, WIDTH)


def _diff_sample_kernel(slopes_ref, q_ref, k_ref, v_ref, pk_ref, pv_ref, lamq_ref, lamk_ref, g_ref, o_ref, *,
                        lam_init):
    t = q_ref.shape[0]
    lam = _lambda_value(lamq_ref, lamk_ref, lam_init)
    g = g_ref[...]
    past_col = lax.broadcasted_iota(jnp.int32, (1, PAST_LEN), 1)
    row = lax.broadcasted_iota(jnp.int32, (t, t), 0)
    col = lax.broadcasted_iota(jnp.int32, (t, t), 1)
    past_dist = (past_col - PAST_LEN).astype(F32)
    new_dist = (row - jnp.abs(row - col)).astype(F32)
    for h in range(HEADS):
        cols = slice(h * HEAD_DIM, (h + 1) * HEAD_DIM)
        slope = slopes_ref[h]
        k_past = pk_ref[:, h, :].astype(BF16)
        v_past = pv_ref[:, h, :].astype(BF16)
        k_new = k_ref[:, cols].astype(BF16)
        v_new = v_ref[:, cols].astype(BF16)
        outs = []
        for qz in _split_halves(q_ref[:, cols]):
            s_past = _dot_nt(qz, k_past) + slope * past_dist
            s_new = _dot_nt(qz, k_new) + slope * new_dist
            m = jnp.maximum(jnp.max(s_past, axis=-1, keepdims=True), jnp.max(s_new, axis=-1, keepdims=True))
            p_past = jnp.exp2(s_past - m)
            p_new = jnp.exp2(s_new - m)
            total = jnp.sum(p_past, axis=-1, keepdims=True) + jnp.sum(p_new, axis=-1, keepdims=True)
            outs.append((_dot(p_past.astype(BF16), v_past) + _dot(p_new.astype(BF16), v_new)) / total)
        o_ref[:, cols] = _diff_finish(outs[0], outs[1], lam, g, lam_init).astype(o_ref.dtype)


def _diff_attention_sample(q, k, v, cache_k, cache_v, layer, lam_params, g, slopes, lam_init, batch):
    m = q.shape[0]
    t = m // batch
    assert t == CHUNK and PAST_LEN % CHUNK == 0
    q = q.reshape(batch, t, WIDTH)
    k, v = (a.reshape(DEPTH, batch, t, WIDTH) for a in (k, v))
    new_spec = pl.BlockSpec((None, t, WIDTH), lambda b: (b, 0, 0))
    new_kv_spec = pl.BlockSpec((None, None, t, WIDTH), lambda b: (layer, b, 0, 0))
    past_spec = pl.BlockSpec((None, None, PAST_LEN, HEADS, HEAD_DIM), lambda b: (layer, b, 0, 0, 0))
    lam_spec = pl.BlockSpec((2, DIFF_QK_DIM), lambda b: (0, 0))
    out = pl.pallas_call(
        functools.partial(_diff_sample_kernel, lam_init=lam_init),
        grid=(batch,),
        in_specs=[pl.BlockSpec(memory_space=pltpu.SMEM), new_spec, new_kv_spec, new_kv_spec, past_spec, past_spec,
                  lam_spec, lam_spec, pl.BlockSpec((1, HEAD_DIM), lambda b: (0, 0))],
        out_specs=new_spec,
        out_shape=jax.ShapeDtypeStruct((batch, t, WIDTH), BF16),
        compiler_params=_params("arbitrary"),
        name="diff_attention_sample",
    )(slopes, q, k, v, cache_k, cache_v, *lam_params, g)
    return out.reshape(m, WIDTH)


def _sigmoid(x):
    return 1.0 / (1.0 + jnp.exp(-x))


def _merge_kernel(osb_ref, odf_ref, gates_ref, x_ref, wsb_ref, wdf_ref, wout_ref, o_ref, merged_ref):
    osb = osb_ref[...]
    odf = odf_ref[...]
    for c in range(D_MODEL // MERGE_COL_CHUNK):
        cols = slice(c * MERGE_COL_CHUNK, (c + 1) * MERGE_COL_CHUNK)
        gcols = slice(D_MODEL + c * MERGE_COL_CHUNK, D_MODEL + (c + 1) * MERGE_COL_CHUNK)
        g_sb = _sigmoid(gates_ref[:, cols].astype(F32))
        g_df = _sigmoid(gates_ref[:, gcols].astype(F32))
        merged = g_sb * _dot(osb, wsb_ref[:, cols]) + g_df * _dot(odf, wdf_ref[:, cols])
        merged_ref[:, cols] = merged.astype(BF16)
    o_ref[...] = x_ref[...] + _dot(merged_ref[...], wout_ref[...])


def _merge_project(o_sb, o_df, gates, x, w_sb, w_df, w_out):
    m = x.shape[0]
    tm = min(M_TILE, m)
    row_spec = lambda width: pl.BlockSpec((tm, width), lambda i: (i, 0))
    whole = lambda a: pl.BlockSpec(a.shape, lambda i: (0, 0), pipeline_mode=pl.Buffered(1))
    return pl.pallas_call(
        _merge_kernel,
        grid=(m // tm,),
        in_specs=[row_spec(WIDTH), row_spec(WIDTH), row_spec(2 * D_MODEL), row_spec(D_MODEL),
                  whole(w_sb), whole(w_df), whole(w_out)],
        out_specs=row_spec(D_MODEL),
        out_shape=jax.ShapeDtypeStruct((m, D_MODEL), F32),
        scratch_shapes=[pltpu.VMEM((tm, D_MODEL), BF16)],
        compiler_params=_params("arbitrary"),
        name="merge_project",
    )(o_sb, o_df, gates, x, w_sb, w_df, w_out)


def _mlp_kernel(x_ref, g_ref, wup_ref, wdown_ref, gf_ref, o_ref, h_ref, acc_ref, *, final_norm):
    f = pl.program_id(1)

    @pl.when(f == 0)
    def _():
        h_ref[...] = _rmsnorm_rows(x_ref[...], g_ref[...]).astype(BF16)
        acc_ref[...] = jnp.zeros_like(acc_ref)

    u = jnp.maximum(_dot(h_ref[...], wup_ref[...]), 0.0)
    acc_ref[...] += _dot((u * u).astype(BF16), wdown_ref[...])

    @pl.when(f == pl.num_programs(1) - 1)
    def _():
        y = x_ref[...] + acc_ref[...]
        if final_norm:
            y = _rmsnorm_rows(y, gf_ref[...])
        o_ref[...] = y


def _mlp(x, g, w_up, w_down, g_final, final_norm):
    m = x.shape[0]
    tm = min(M_TILE, m)
    vec_spec = pl.BlockSpec((1, D_MODEL), lambda i, f: (0, 0))
    return pl.pallas_call(
        functools.partial(_mlp_kernel, final_norm=final_norm),
        grid=(m // tm, D_FF // F_TILE),
        in_specs=[pl.BlockSpec((tm, D_MODEL), lambda i, f: (i, 0)), vec_spec,
                  pl.BlockSpec((D_MODEL, F_TILE), lambda i, f: (0, f)),
                  pl.BlockSpec((F_TILE, D_MODEL), lambda i, f: (f, 0)), vec_spec],
        out_specs=pl.BlockSpec((tm, D_MODEL), lambda i, f: (i, 0)),
        out_shape=jax.ShapeDtypeStruct((m, D_MODEL), F32),
        scratch_shapes=[pltpu.VMEM((tm, D_MODEL), BF16), pltpu.VMEM((tm, D_MODEL), F32)],
        compiler_params=_params("arbitrary", "arbitrary"),
        name="mlp",
    )(x, g, w_up, w_down, g_final)


def _trunk(x, caches, weights, slopes):
    batch, t, _ = x.shape
    x = x.reshape(batch * t, D_MODEL)
    kv_rows = tuple(jnp.zeros((DEPTH, batch * t, WIDTH), F32) for _ in range(4))
    for layer in range(DEPTH):
        w = {name: value[layer] for name, value in weights.items() if name != "final_norm_g"}
        lam_init = 0.8 - 0.6 * math.exp(-0.3 * layer)
        q_sb, k_sb, v_sb, q_df, k_df, v_df, gates = _in_projection(x, w["norm1_g"][None], w["w_in"], layer,
                                                                   kv_rows)
        kv_rows = (k_sb, v_sb, k_df, v_df)
        lam_params = (jnp.stack([w["lambda_q1"], w["lambda_q2"]]), jnp.stack([w["lambda_k1"], w["lambda_k2"]]))
        g_diff = w["diff_norm_g"][None]
        if caches is None:
            o_sb = _sb_attention_prompt(q_sb, k_sb, v_sb, layer, batch)
            o_df = _diff_attention_prompt(q_df, k_df, v_df, layer, lam_params, g_diff, slopes, lam_init, batch)
        else:
            c_sb_k, c_sb_v, c_df_k, c_df_v = caches
            o_sb = _sb_attention_sample(q_sb, k_sb, v_sb, c_sb_k, c_sb_v, layer, batch)
            o_df = _diff_attention_sample(q_df, k_df, v_df, c_df_k, c_df_v, layer, lam_params, g_diff,
                                          slopes, lam_init, batch)
        x = _merge_project(o_sb, o_df, gates, x, w["w_branch_sb"], w["w_branch_diff"], w["w_out"])
        x = _mlp(x, w["norm2_g"][None], w["w_up"], w["w_down"], weights["final_norm_g"][None],
                 final_norm=(layer == DEPTH - 1))
    stacked = tuple(r.reshape(DEPTH, batch, t, HEADS, HEAD_DIM) for r in kv_rows)
    return x.reshape(batch, t, D_MODEL), stacked


def kernel(x_prompt, x_sample, cache_sb_k, cache_sb_v, cache_diff_k, cache_diff_v, norm1_g, w_in, lambda_q1, lambda_k1, lambda_q2, lambda_k2, diff_norm_g, w_branch_sb, w_branch_diff, w_out, norm2_g, w_up, w_down, final_norm_g):
    weights = dict(norm1_g=norm1_g, w_in=w_in.astype(BF16), lambda_q1=lambda_q1, lambda_k1=lambda_k1,
                   lambda_q2=lambda_q2, lambda_k2=lambda_k2, diff_norm_g=diff_norm_g,
                   w_branch_sb=w_branch_sb.astype(BF16), w_branch_diff=w_branch_diff.astype(BF16),
                   w_out=w_out.astype(BF16), norm2_g=norm2_g, w_up=w_up.astype(BF16),
                   w_down=w_down.astype(BF16), final_norm_g=final_norm_g)
    slopes = jnp.exp2(-8.0 * jnp.arange(1, HEADS + 1, dtype=F32) / HEADS) * LOG2E
    caches = (cache_sb_k, cache_sb_v, cache_diff_k, cache_diff_v)
    y_prompt, rows_p = _trunk(x_prompt, None, weights, slopes)
    y_sample, rows_s = _trunk(x_sample, caches, weights, slopes)
    return (y_prompt, y_sample) + rows_p + rows_s
```
